```python
import jax, jax.numpy as jnp
from jax import lax
import numpy as np

D_MODEL = 2048
BATCH = 4
SEQ = 2048
DEPTH = 1
DEC_BATCH = 32
DEC_SEQ = 8
PAST_LEN = 16384
PAGE_SIZE = 128

N_META = 16
N_HEADS = 32
N_KV_HEADS = 8
HEAD_DIM = 64
GROUP = N_HEADS // N_KV_HEADS
WINDOW = 128
BLOCK = 128
ATTN_WIDTH = N_HEADS * HEAD_DIM
KV_WIDTH = N_KV_HEADS * HEAD_DIM
POOL_WIDTH = D_MODEL // 2
POOL_WINDOWS = (2, 4, 8, 16)
N_POOL_GROUPS = len(POOL_WINDOWS)
POOL_GROUP_DIM = POOL_WIDTH // N_POOL_GROUPS
POOL_BUF = max(POOL_WINDOWS) - 1
N_EXPERTS = 32
TOP_K = 4
D_FF = D_MODEL
SWIGLU_LIMIT = 7.0
SWIGLU_ALPHA = 1.702
MOE_BLOCK = 128
EPS = 1e-5
Q_END = ATTN_WIDTH
K_END = Q_END + KV_WIDTH
V_END = K_END + KV_WIDTH
U_END = V_END + POOL_WIDTH
IN_COLS = U_END + 2 * D_MODEL

kernel_name = "hybrid_swa_pool_moe_decode_step"


def rmsnorm(x, g):
    xf = x.astype(jnp.float32)
    xf = xf * lax.rsqrt(jnp.mean(xf * xf, axis=-1, keepdims=True) + EPS)
    return (xf * g.astype(jnp.float32)).astype(x.dtype)


def alibi_slopes():
    return jnp.asarray(2.0 ** (-8.0 * np.arange(1, N_HEADS + 1) / N_HEADS), dtype=jnp.float32)


def project(h, w_in):
    b, t = h.shape[:2]
    p = jnp.einsum("btd,dc->btc", h, w_in)
    q = p[..., :Q_END].reshape(b, t, N_KV_HEADS, GROUP, HEAD_DIM)
    k = p[..., Q_END:K_END].reshape(b, t, N_KV_HEADS, HEAD_DIM)
    v = p[..., K_END:V_END].reshape(b, t, N_KV_HEADS, HEAD_DIM)
    u = p[..., V_END:U_END]
    a_gate = p[..., U_END:U_END + D_MODEL]
    p_gate = p[..., U_END + D_MODEL:]
    return q, k, v, u, a_gate, p_gate


def band_attention(q, k, v, q_pos, k_pos, sinks):
    s = jnp.einsum("bnqhgd,bnkhd->bnhgqk", q, k, preferred_element_type=jnp.float32) * (HEAD_DIM ** -0.5)
    dist = q_pos[:, :, None] - k_pos[:, None, :]
    valid = (dist >= 0) & (dist <= WINDOW) & (k_pos[:, None, :] >= 0)
    slopes = alibi_slopes().reshape(N_KV_HEADS, GROUP)
    s = s - slopes[None, None, :, :, None, None] * dist.astype(jnp.float32)[None, :, None, None]
    s = jnp.where(valid[None, :, None, None], s, -jnp.inf)
    sink = sinks.astype(jnp.float32).reshape(N_KV_HEADS, GROUP)[None, None, :, :, None, None]
    m = jnp.maximum(jnp.max(s, axis=-1, keepdims=True), sink)
    p = jnp.exp(s - m)
    p = p / (jnp.sum(p, axis=-1, keepdims=True) + jnp.exp(sink - m))
    return jnp.einsum("bnhgqk,bnkhd->bnqhgd", p.astype(v.dtype), v)


def prompt_attention(q, k, v, sinks):
    b, length = q.shape[:2]
    pad = BLOCK - N_META
    lp = length + pad
    nb = lp // BLOCK
    qb = jnp.pad(q, ((0, 0), (pad, 0), (0, 0), (0, 0), (0, 0))).reshape(b, nb, BLOCK, N_KV_HEADS, GROUP, HEAD_DIM)

    def kv_blocks(t):
        tb = jnp.pad(t, ((0, 0), (pad, 0), (0, 0), (0, 0))).reshape(b, nb, BLOCK, N_KV_HEADS, HEAD_DIM)
        prev = jnp.pad(tb[:, :-1], ((0, 0), (1, 0), (0, 0), (0, 0), (0, 0)))
        return jnp.concatenate([prev, tb], axis=2)

    pos = (jnp.arange(lp, dtype=jnp.int32) - pad).reshape(nb, BLOCK)
    k_pos = jnp.concatenate([pos - BLOCK, pos], axis=1)
    o = band_attention(qb, kv_blocks(k), kv_blocks(v), pos, k_pos, sinks)
    return o.reshape(b, lp, ATTN_WIDTH)[:, pad:]


def sample_attention(q, k_all, v_all, q_pos, sinks):
    b, s = q.shape[:2]
    k_pos = jnp.concatenate([PAST_LEN - WINDOW + jnp.arange(WINDOW, dtype=jnp.int32), q_pos])
    o = band_attention(q[:, None], k_all[:, None], v_all[:, None], q_pos[None], k_pos[None], sinks)
    return o.reshape(b, s, ATTN_WIDTH)


def multiscale_pool(u_ext, pos_new, w_pool_mix, pool_scale):
    b = u_ext.shape[0]
    t = u_ext.shape[1] - POOL_BUF
    uf = u_ext.astype(jnp.float32)
    csum = jnp.pad(jnp.cumsum(uf, axis=1), ((0, 0), (1, 0), (0, 0)))
    end = csum[:, POOL_BUF + 1:]
    own = uf[:, POOL_BUF:]
    outs = []
    for g, w in enumerate(POOL_WINDOWS):
        sl = slice(g * POOL_GROUP_DIM, (g + 1) * POOL_GROUP_DIM)
        start = csum[:, POOL_BUF + 1 - w:POOL_BUF + 1 - w + t, sl]
        cnt = jnp.minimum(pos_new + 1, w).astype(jnp.float32)[None, :, None]
        outs.append((end[..., sl] - start) / cnt - own[..., sl])
    z = jnp.stack(outs, axis=2).astype(u_ext.dtype)
    z = jnp.einsum("btgc,gcd->btgd", z, w_pool_mix).reshape(b, t, POOL_WIDTH)
    return z * pool_scale


def moe(h, w_router, b_router, w_gate_up, b_gate_up, w_down, b_down):
    b, t, d = h.shape
    xt = h.reshape(b * t, d)
    n_tok = b * t
    logits = jnp.einsum("td,de->te", xt.astype(jnp.float32), w_router.astype(jnp.float32)) + b_router.astype(jnp.float32)
    top_val, top_idx = lax.top_k(logits, TOP_K)
    gate = jax.nn.softmax(top_val, axis=-1)
    n_assign = n_tok * TOP_K
    e_flat = top_idx.reshape(-1).astype(jnp.int32)
    t_flat = jnp.repeat(jnp.arange(n_tok, dtype=jnp.int32), TOP_K, total_repeat_length=n_assign)
    w_flat = gate.reshape(-1)
    order = jnp.argsort(e_flat)
    e_s, t_s, w_s = e_flat[order], t_flat[order], w_flat[order]
    counts = jnp.zeros((N_EXPERTS,), jnp.int32).at[e_flat].add(1)
    starts = jnp.cumsum(counts) - counts
    padded = (counts + MOE_BLOCK - 1) // MOE_BLOCK * MOE_BLOCK
    pends = jnp.cumsum(padded)
    pstarts = pends - padded
    dest = pstarts[e_s] + (jnp.arange(n_assign, dtype=jnp.int32) - starts[e_s])
    n_blocks = (n_assign + MOE_BLOCK - 1) // MOE_BLOCK + N_EXPERTS
    cap = n_blocks * MOE_BLOCK
    tok = jnp.full((cap,), n_tok, jnp.int32).at[dest].set(t_s)
    wt = jnp.zeros((cap,), jnp.float32).at[dest].set(w_s)
    block_start = jnp.arange(n_blocks, dtype=jnp.int32) * MOE_BLOCK
    block_exp = jnp.minimum(jnp.sum(block_start[:, None] >= pends[None, :], axis=1), N_EXPERTS - 1)
    x_pad = jnp.concatenate([xt, jnp.zeros((1, d), xt.dtype)], axis=0)

    def run_block(args):
        e, idx, w = args
        xb = x_pad[idx]
        gu = xb @ w_gate_up[e] + b_gate_up[e]
        x_glu = jnp.minimum(gu[:, :D_FF], SWIGLU_LIMIT)
        x_lin = jnp.clip(gu[:, D_FF:], -SWIGLU_LIMIT, SWIGLU_LIMIT)
        act = x_glu * jax.nn.sigmoid(SWIGLU_ALPHA * x_glu) * (x_lin + 1)
        yb = act @ w_down[e] + b_down[e]
        return (yb.astype(jnp.float32) * w[:, None]).astype(xt.dtype)

    yb = lax.map(run_block, (block_exp, tok.reshape(n_blocks, MOE_BLOCK), wt.reshape(n_blocks, MOE_BLOCK)))
    y = jax.ops.segment_sum(yb.reshape(cap, d), tok, num_segments=n_tok + 1)[:n_tok]
    return y.reshape(b, t, d)


def finish_layer(x, o_attn, o_pool, a_gate, p_gate, w_br_attn, w_br_pool, w_out, g_ffn,
                 w_router, b_router, w_gate_up, b_gate_up, w_down, b_down):
    ya = jnp.einsum("btc,cd->btd", o_attn, w_br_attn)
    yp = jnp.einsum("btc,cd->btd", o_pool, w_br_pool)
    merged = jax.nn.sigmoid(a_gate) * ya + jax.nn.sigmoid(p_gate) * yp
    x = x + jnp.einsum("btd,de->bte", merged, w_out)
    return x + moe(rmsnorm(x, g_ffn), w_router, b_router, w_gate_up, b_gate_up, w_down, b_down)


def setup_inputs(seed: int = 0) -> dict:
    key = jax.random.key(seed)
    ks = jax.random.split(key, 22)

    def nrm(k, shape, scale):
        return scale * jax.random.normal(k, shape, jnp.float32)

    return {
        "x_prompt": nrm(ks[0], (BATCH, SEQ, D_MODEL), 1.0),
        "x_sample": nrm(ks[1], (DEC_BATCH, DEC_SEQ, D_MODEL), 1.0),
        "cache_k": nrm(ks[2], (DEPTH, DEC_BATCH, WINDOW, N_KV_HEADS, HEAD_DIM), 1.0),
        "cache_v": nrm(ks[3], (DEPTH, DEC_BATCH, WINDOW, N_KV_HEADS, HEAD_DIM), 1.0),
        "state_pool": nrm(ks[4], (DEPTH, DEC_BATCH, POOL_BUF, POOL_WIDTH), 1.0),
        "meta_tokens": nrm(ks[5], (N_META, D_MODEL), 1.0),
        "g_mix": 1.0 + nrm(ks[6], (DEPTH, D_MODEL), 0.01),
        "w_in": nrm(ks[7], (DEPTH, D_MODEL, IN_COLS), D_MODEL ** -0.5),
        "sinks": nrm(ks[8], (DEPTH, N_HEADS), 1.0),
        "w_pool_mix": nrm(ks[9], (DEPTH, N_POOL_GROUPS, POOL_GROUP_DIM, POOL_GROUP_DIM), POOL_GROUP_DIM ** -0.5),
        "pool_scale": 1.0 + nrm(ks[10], (DEPTH, POOL_WIDTH), 0.1),
        "w_br_attn": nrm(ks[11], (DEPTH, ATTN_WIDTH, D_MODEL), ATTN_WIDTH ** -0.5),
        "w_br_pool": nrm(ks[12], (DEPTH, POOL_WIDTH, D_MODEL), POOL_WIDTH ** -0.5),
        "w_out": nrm(ks[13], (DEPTH, D_MODEL, D_MODEL), D_MODEL ** -0.5),
        "g_ffn": 1.0 + nrm(ks[14], (DEPTH, D_MODEL), 0.01),
        "w_router": nrm(ks[15], (DEPTH, D_MODEL, N_EXPERTS), D_MODEL ** -0.5),
        "b_router": nrm(ks[16], (DEPTH, N_EXPERTS), 0.01),
        "w_gate_up": nrm(ks[17], (DEPTH, N_EXPERTS, D_MODEL, 2 * D_FF), D_MODEL ** -0.5),
        "b_gate_up": nrm(ks[18], (DEPTH, N_EXPERTS, 2 * D_FF), 0.01),
        "w_down": nrm(ks[19], (DEPTH, N_EXPERTS, D_FF, D_MODEL), D_FF ** -0.5),
        "b_down": nrm(ks[20], (DEPTH, N_EXPERTS, D_MODEL), 0.01),
        "g_final": 1.0 + nrm(ks[21], (D_MODEL,), 0.01),
    }


def reference(x_prompt, x_sample, cache_k, cache_v, state_pool, meta_tokens, g_mix, w_in, sinks,
              w_pool_mix, pool_scale, w_br_attn, w_br_pool, w_out, g_ffn, w_router, b_router,
              w_gate_up, b_gate_up, w_down, b_down, g_final):
    b_p = x_prompt.shape[0]
    n_prompt = x_prompt.shape[1] + N_META
    n_new = x_sample.shape[1]
    meta = jnp.broadcast_to(meta_tokens.astype(x_prompt.dtype)[None], (b_p, N_META, D_MODEL))
    xp = jnp.concatenate([meta, x_prompt], axis=1)
    xs = x_sample
    pos_p = jnp.arange(n_prompt, dtype=jnp.int32)
    pos_s = PAST_LEN + jnp.arange(n_new, dtype=jnp.int32)
    kp_l, vp_l, up_l, ks_l, vs_l, us_l = [], [], [], [], [], []
    for l in range(DEPTH):
        ffn_w = (w_br_attn[l], w_br_pool[l], w_out[l], g_ffn[l], w_router[l], b_router[l],
                 w_gate_up[l], b_gate_up[l], w_down[l], b_down[l])
        q, k, v, u, a_gate, p_gate = project(rmsnorm(xp, g_mix[l]), w_in[l])
        o_a = prompt_attention(q, k, v, sinks[l])
        u_ext = jnp.pad(u, ((0, 0), (POOL_BUF, 0), (0, 0)))
        o_p = multiscale_pool(u_ext, pos_p, w_pool_mix[l], pool_scale[l])
        xp = finish_layer(xp, o_a, o_p, a_gate, p_gate, *ffn_w)
        kp_l.append(k[:, -WINDOW:])
        vp_l.append(v[:, -WINDOW:])
        up_l.append(u[:, -POOL_BUF:])
        q, k, v, u, a_gate, p_gate = project(rmsnorm(xs, g_mix[l]), w_in[l])
        k_all = jnp.concatenate([cache_k[l].astype(k.dtype), k], axis=1)
        v_all = jnp.concatenate([cache_v[l].astype(v.dtype), v], axis=1)
        o_a = sample_attention(q, k_all, v_all, pos_s, sinks[l])
        u_ext = jnp.concatenate([state_pool[l].astype(u.dtype), u], axis=1)
        o_p = multiscale_pool(u_ext, pos_s, w_pool_mix[l], pool_scale[l])
        xs = finish_layer(xs, o_a, o_p, a_gate, p_gate, *ffn_w)
        ks_l.append(k_all[:, -WINDOW:])
        vs_l.append(v_all[:, -WINDOW:])
        us_l.append(u_ext[:, -POOL_BUF:])
    y_prompt = rmsnorm(xp, g_final)[:, N_META:]
    y_sample = rmsnorm(xs, g_final)
    return (y_prompt, y_sample, jnp.stack(kp_l), jnp.stack(vp_l), jnp.stack(up_l),
            jnp.stack(ks_l), jnp.stack(vs_l), jnp.stack(us_l))
```

```python
import functools

import jax
import jax.numpy as jnp
from jax import lax
from jax.experimental import pallas as pl
from jax.experimental.pallas import tpu as pltpu

F32 = jnp.float32
BF16 = jnp.bfloat16

D_MODEL = 2048
N_META = 16
N_HEADS = 32
N_KV_HEADS = 8
HEAD_DIM = 64
GROUP = N_HEADS // N_KV_HEADS
WINDOW = 128
ATTN_WIDTH = N_HEADS * HEAD_DIM
KV_WIDTH = N_KV_HEADS * HEAD_DIM
POOL_WIDTH = D_MODEL // 2
POOL_WINDOWS = (2, 4, 8, 16)
POOL_GROUP_DIM = POOL_WIDTH // len(POOL_WINDOWS)
POOL_BUF = max(POOL_WINDOWS) - 1
N_EXPERTS = 32
TOP_K = 4
D_FF = D_MODEL
SWIGLU_LIMIT = 7.0
SWIGLU_ALPHA = 1.702
EPS = 1e-5
PAST_LEN = 16384
IN_COLS = ATTN_WIDTH + 2 * KV_WIDTH + POOL_WIDTH + 2 * D_MODEL

LANES = 128
SUBLANES = 8
VMEM_LIMIT_BYTES = 56 * 1024 * 1024

ATTN_TILE = WINDOW
PROJ_TM = 512
PROJ_TN = 1024
ROW_TM = 256
MOE_SUB = 256
MOE_MAXM = 6 * MOE_SUB
MOE_F = 256
MOE_NF = D_FF // MOE_F
NEG_BIG = -1e30

_SLOPES = tuple(float(2.0 ** (-8.0 * (i + 1) / N_HEADS)) for i in range(N_HEADS))


def _cparams(sem, vmem=VMEM_LIMIT_BYTES):
    return pltpu.CompilerParams(dimension_semantics=sem, vmem_limit_bytes=vmem)


def _inproj_kernel(x_ref, g_ref, w_ref, o_ref, wbf_ref):
    @pl.when(pl.program_id(1) == 0)
    def _():
        wbf_ref[...] = w_ref[...].astype(BF16)

    x = x_ref[...]
    ms = jnp.mean(x * x, axis=-1, keepdims=True)
    h = (x * lax.rsqrt(ms + EPS) * g_ref[...]).astype(BF16)
    o_ref[...] = jnp.dot(h, wbf_ref[...], preferred_element_type=F32)


def _inproj(x_all, g_mix, w_in):
    n_rows = x_all.shape[0]
    grid = (IN_COLS // PROJ_TN, n_rows // PROJ_TM)
    return pl.pallas_call(
        _inproj_kernel,
        grid=grid,
        in_specs=[
            pl.BlockSpec((PROJ_TM, D_MODEL), lambda n, m: (m, 0)),
            pl.BlockSpec((1, D_MODEL), lambda n, m: (0, 0)),
            pl.BlockSpec((D_MODEL, PROJ_TN), lambda n, m: (0, n)),
        ],
        out_specs=pl.BlockSpec((PROJ_TM, PROJ_TN), lambda n, m: (m, n)),
        out_shape=jax.ShapeDtypeStruct((n_rows, IN_COLS), F32),
        scratch_shapes=[pltpu.VMEM((D_MODEL, PROJ_TN), BF16)],
        compiler_params=_cparams(("arbitrary", "arbitrary")),
        name="inproj",
    )(x_all, g_mix.reshape(1, D_MODEL), w_in)


def _softmax_pv(s, sink, vh):
    m = jnp.maximum(jnp.max(s, axis=-1, keepdims=True), sink)
    p = jnp.exp(s - m)
    den = jnp.sum(p, axis=-1, keepdims=True) + jnp.exp(sink - m)
    o = jnp.dot(p.astype(BF16), vh, preferred_element_type=F32)
    return o / den


def _pool_features(ext, n_halo):
    outs = []
    for g, w in enumerate(POOL_WINDOWS):
        e = ext[:, g * POOL_GROUP_DIM:(g + 1) * POOL_GROUP_DIM]
        s = e
        shift = 1
        while shift < w:
            s = s + pltpu.roll(s, shift, axis=0)
            shift *= 2
        outs.append(s[n_halo:] * (1.0 / w) - e[n_halo:])
    return jnp.concatenate(outs, axis=1)


def _attn_prompt_kernel(sink_ref, q_ref, ko_ref, kp_ref, vo_ref, vp_ref, uo_ref, up_ref,
                        o_ref, z_ref):
    j = pl.program_id(1)
    t = ATTN_TILE
    r = lax.broadcasted_iota(jnp.int32, (t, 2 * t), 0)
    c = lax.broadcasted_iota(jnp.int32, (t, 2 * t), 1)
    dist = r + t - c
    prev_ok = jnp.logical_or(j > 0, c >= t - N_META)
    valid = (dist >= 0) & (dist <= WINDOW) & prev_ok
    distf = dist.astype(F32)

    q = (q_ref[...] * (HEAD_DIM ** -0.5)).astype(BF16)
    k = jnp.concatenate([kp_ref[...], ko_ref[...]], axis=0).astype(BF16)
    v = jnp.concatenate([vp_ref[...], vo_ref[...]], axis=0).astype(BF16)
    for hd in range(N_HEADS):
        kvh = hd // GROUP
        kh = k[:, kvh * HEAD_DIM:(kvh + 1) * HEAD_DIM]
        vh = v[:, kvh * HEAD_DIM:(kvh + 1) * HEAD_DIM]
        qh = q[:, hd * HEAD_DIM:(hd + 1) * HEAD_DIM]
        s = lax.dot_general(qh, kh, (((1,), (1,)), ((), ())), preferred_element_type=F32)
        s = jnp.where(valid, s - _SLOPES[hd] * distf, NEG_BIG)
        o = _softmax_pv(s, sink_ref[hd], vh)
        o_ref[:, hd * HEAD_DIM:(hd + 1) * HEAD_DIM] = o.astype(o_ref.dtype)

    ext = jnp.concatenate([up_ref[...], uo_ref[...]], axis=0)
    z_ref[...] = _pool_features(ext, N_META).astype(z_ref.dtype)


def _attn_prompt(p_all, sinks, batch, seq):
    t = ATTN_TILE
    tiles = seq // t
    meta_blk = p_all.shape[0] // t - 1
    q_w, kv_w, u_w = ATTN_WIDTH, KV_WIDTH, POOL_WIDTH
    kcol, vcol, ucol = ATTN_WIDTH // kv_w, ATTN_WIDTH // kv_w + 1, (ATTN_WIDTH + 2 * kv_w) // u_w
    halo_per_tile = t // N_META

    def own(b, j):
        return b * tiles + j

    def prev(b, j):
        return jnp.where(j > 0, b * tiles + j - 1, meta_blk)

    return pl.pallas_call(
        _attn_prompt_kernel,
        grid=(batch, tiles),
        in_specs=[
            pl.BlockSpec(memory_space=pltpu.SMEM),
            pl.BlockSpec((t, q_w), lambda b, j: (own(b, j), 0)),
            pl.BlockSpec((t, kv_w), lambda b, j: (own(b, j), kcol)),
            pl.BlockSpec((t, kv_w), lambda b, j: (prev(b, j), kcol)),
            pl.BlockSpec((t, kv_w), lambda b, j: (own(b, j), vcol)),
            pl.BlockSpec((t, kv_w), lambda b, j: (prev(b, j), vcol)),
            pl.BlockSpec((t, u_w), lambda b, j: (own(b, j), ucol)),
            pl.BlockSpec((N_META, u_w), lambda b, j: (prev(b, j) * halo_per_tile + halo_per_tile - 1, ucol)),
        ],
        out_specs=[
            pl.BlockSpec((t, q_w), lambda b, j: (own(b, j), 0)),
            pl.BlockSpec((t, u_w), lambda b, j: (own(b, j), 0)),
        ],
        out_shape=[
            jax.ShapeDtypeStruct((batch * seq, q_w), BF16),
            jax.ShapeDtypeStruct((batch * seq, u_w), BF16),
        ],
        compiler_params=_cparams(("arbitrary", "arbitrary")),
        name="attn_prompt",
    )(sinks, p_all, p_all, p_all, p_all, p_all, p_all, p_all)


def _attn_sample_kernel(sink_ref, q_ref, kn_ref, vn_ref, un_ref, ck_ref, cv_ref, sp_ref,
                        o_ref, z_ref):
    n_new = q_ref.shape[0]
    n_keys = WINDOW + n_new
    rows = GROUP * n_new
    r = lax.broadcasted_iota(jnp.int32, (rows, n_keys), 0)
    c = lax.broadcasted_iota(jnp.int32, (rows, n_keys), 1)
    g_of_row = r // n_new
    dist = (r - g_of_row * n_new) + WINDOW - c
    valid = (dist >= 0) & (dist <= WINDOW)
    distf = dist.astype(F32)
    g_col = lax.broadcasted_iota(jnp.int32, (rows, 1), 0) // n_new

    q = (q_ref[...] * (HEAD_DIM ** -0.5)).astype(BF16)
    k = jnp.concatenate([ck_ref[0], kn_ref[...]], axis=0).astype(BF16)
    v = jnp.concatenate([cv_ref[0], vn_ref[...]], axis=0).astype(BF16)
    for kvh in range(N_KV_HEADS):
        kh = k[:, kvh * HEAD_DIM:(kvh + 1) * HEAD_DIM]
        vh = v[:, kvh * HEAD_DIM:(kvh + 1) * HEAD_DIM]
        heads = [kvh * GROUP + g for g in range(GROUP)]
        qg = jnp.concatenate([q[:, hd * HEAD_DIM:(hd + 1) * HEAD_DIM] for hd in heads], axis=0)
        slope = jnp.zeros((rows, 1), F32)
        sink = jnp.zeros((rows, 1), F32)
        for g, hd in enumerate(heads):
            slope = jnp.where(g_col == g, _SLOPES[hd], slope)
            sink = jnp.where(g_col == g, sink_ref[hd], sink)
        s = lax.dot_general(qg, kh, (((1,), (1,)), ((), ())), preferred_element_type=F32)
        s = jnp.where(valid, s - slope * distf, NEG_BIG)
        o = _softmax_pv(s, sink, vh)
        for g, hd in enumerate(heads):
            o_ref[:, hd * HEAD_DIM:(hd + 1) * HEAD_DIM] = o[g * n_new:(g + 1) * n_new].astype(o_ref.dtype)

    ext = jnp.concatenate([sp_ref[0], un_ref[...]], axis=0)
    z_ref[...] = _pool_features(ext, sp_ref.shape[1]).astype(z_ref.dtype)


def _attn_sample(p_all, sinks, cache_k, cache_v, state_pad, row0, dec_batch, dec_seq):
    q_w, kv_w, u_w = ATTN_WIDTH, KV_WIDTH, POOL_WIDTH
    kcol, vcol, ucol = ATTN_WIDTH // kv_w, ATTN_WIDTH // kv_w + 1, (ATTN_WIDTH + 2 * kv_w) // u_w
    blk0 = row0 // dec_seq
    n_halo = state_pad.shape[1]
    return pl.pallas_call(
        _attn_sample_kernel,
        grid=(dec_batch,),
        in_specs=[
            pl.BlockSpec(memory_space=pltpu.SMEM),
            pl.BlockSpec((dec_seq, q_w), lambda b: (blk0 + b, 0)),
            pl.BlockSpec((dec_seq, kv_w), lambda b: (blk0 + b, kcol)),
            pl.BlockSpec((dec_seq, kv_w), lambda b: (blk0 + b, vcol)),
            pl.BlockSpec((dec_seq, u_w), lambda b: (blk0 + b, ucol)),
            pl.BlockSpec((1, WINDOW, kv_w), lambda b: (b, 0, 0)),
            pl.BlockSpec((1, WINDOW, kv_w), lambda b: (b, 0, 0)),
            pl.BlockSpec((1, n_halo, u_w), lambda b: (b, 0, 0)),
        ],
        out_specs=[
            pl.BlockSpec((dec_seq, q_w), lambda b: (b, 0)),
            pl.BlockSpec((dec_seq, u_w), lambda b: (b, 0)),
        ],
        out_shape=[
            jax.ShapeDtypeStruct((dec_batch * dec_seq, q_w), BF16),
            jax.ShapeDtypeStruct((dec_batch * dec_seq, u_w), BF16),
        ],
        compiler_params=_cparams(("arbitrary",)),
        name="attn_sample",
    )(sinks, p_all, p_all, p_all, p_all, cache_k, cache_v, state_pad)


def _finish_kernel(x_ref, oa_ref, z_ref, ag_ref, pg_ref, wpm_ref, ps_ref, wba_ref, wbp_ref,
                   wo_ref, gf_ref, wr_ref, br_ref, x1_ref, h2_ref, idx_ref, gate_ref):
    z = z_ref[...]
    zp = jnp.concatenate(
        [jnp.dot(z[:, g * POOL_GROUP_DIM:(g + 1) * POOL_GROUP_DIM], wpm_ref[g],
                 preferred_element_type=F32) for g in range(len(POOL_WINDOWS))], axis=1)
    zp = (zp * ps_ref[...]).astype(BF16)
    ya = jnp.dot(oa_ref[...], wba_ref[...], preferred_element_type=F32)
    yp = jnp.dot(zp, wbp_ref[...], preferred_element_type=F32)
    merged = jax.nn.sigmoid(ag_ref[...]) * ya + jax.nn.sigmoid(pg_ref[...]) * yp
    x1 = x_ref[...] + jnp.dot(merged.astype(BF16), wo_ref[...], preferred_element_type=F32)
    x1_ref[...] = x1
    ms = jnp.mean(x1 * x1, axis=-1, keepdims=True)
    h2 = x1 * lax.rsqrt(ms + EPS) * gf_ref[...]
    h2_ref[...] = h2

    logits = jnp.dot(h2, wr_ref[...], preferred_element_type=F32,
                     precision=lax.Precision.HIGHEST) + br_ref[...]
    col = lax.broadcasted_iota(jnp.int32, logits.shape, 1).astype(F32)
    vals, idxs = [], []
    for _ in range(TOP_K):
        m = jnp.max(logits, axis=-1, keepdims=True)
        idx = jnp.min(jnp.where(logits == m, col, float(N_EXPERTS)), axis=-1, keepdims=True)
        vals.append(m)
        idxs.append(idx)
        logits = jnp.where(col == idx, -jnp.inf, logits)
    exps = [jnp.exp(v - vals[0]) for v in vals]
    den = exps[0] + exps[1] + exps[2] + exps[3]
    for k in range(TOP_K):
        idx_ref[:, k:k + 1] = idxs[k].astype(jnp.int32)
        gate_ref[:, k:k + 1] = exps[k] / den


def _finish(x_all, o_attn, z, p_all, wpm, pool_scale, wba, wbp, wo, g_ffn, w_router, b_router):
    n_tok = o_attn.shape[0]
    tm = ROW_TM
    acol, pcol = (IN_COLS - 2 * D_MODEL) // D_MODEL, (IN_COLS - D_MODEL) // D_MODEL
    const = pl.Buffered(1)

    def whole(shape):
        nd = len(shape)
        return pl.BlockSpec(shape, lambda i: (0,) * nd, pipeline_mode=const)

    return pl.pallas_call(
        _finish_kernel,
        grid=(n_tok // tm,),
        in_specs=[
            pl.BlockSpec((tm, D_MODEL), lambda i: (i, 0)),
            pl.BlockSpec((tm, ATTN_WIDTH), lambda i: (i, 0)),
            pl.BlockSpec((tm, POOL_WIDTH), lambda i: (i, 0)),
            pl.BlockSpec((tm, D_MODEL), lambda i: (i, acol)),
            pl.BlockSpec((tm, D_MODEL), lambda i: (i, pcol)),
            whole(wpm.shape), whole((1, POOL_WIDTH)), whole(wba.shape), whole(wbp.shape),
            whole(wo.shape), whole((1, D_MODEL)), whole(w_router.shape), whole((1, N_EXPERTS)),
        ],
        out_specs=[
            pl.BlockSpec((tm, D_MODEL), lambda i: (i, 0)),
            pl.BlockSpec((tm, D_MODEL), lambda i: (i, 0)),
            pl.BlockSpec((tm, TOP_K), lambda i: (i, 0)),
            pl.BlockSpec((tm, TOP_K), lambda i: (i, 0)),
        ],
        out_shape=[
            jax.ShapeDtypeStruct((n_tok, D_MODEL), F32),
            jax.ShapeDtypeStruct((n_tok, D_MODEL), F32),
            jax.ShapeDtypeStruct((n_tok, TOP_K), jnp.int32),
            jax.ShapeDtypeStruct((n_tok, TOP_K), F32),
        ],
        compiler_params=_cparams(("arbitrary",)),
        name="finish",
    )(x_all, o_attn, z, p_all, p_all, wpm, pool_scale.reshape(1, POOL_WIDTH), wba, wbp, wo,
      g_ffn.reshape(1, D_MODEL), w_router, b_router.reshape(1, N_EXPERTS))


def _route(top_idx, n_seg, cap):
    n_assign = top_idx.size
    e_flat = top_idx.reshape(-1)
    onehot = (e_flat[:, None] == jnp.arange(N_EXPERTS, dtype=jnp.int32)[None, :]).astype(jnp.int32)
    csum = jnp.cumsum(onehot, axis=0)
    counts = csum[-1]
    padded = (counts + MOE_SUB - 1) // MOE_SUB * MOE_SUB
    pend = jnp.cumsum(padded)
    pstart = pend - padded
    dest = jnp.sum((csum - onehot + pstart[None, :]) * onehot, axis=1)
    tok_sorted = jnp.zeros((cap,), jnp.int32).at[dest].set(
        jnp.arange(n_assign, dtype=jnp.int32) // TOP_K)

    nseg_e = (counts + MOE_MAXM - 1) // MOE_MAXM
    seg_end = jnp.cumsum(nseg_e)
    seg_base = seg_end - nseg_e
    s_ids = jnp.arange(n_seg, dtype=jnp.int32)
    live = s_ids < seg_end[-1]
    last = jnp.maximum(seg_end[-1] - 1, 0)
    s_eff = jnp.where(live, s_ids, last)
    e_of_s = jnp.minimum(jnp.sum(s_eff[:, None] >= seg_end[None, :], axis=1), N_EXPERTS - 1)
    k_in = s_eff - seg_base[e_of_s]
    nrows = jnp.clip(counts[e_of_s] - k_in * MOE_MAXM, 0, MOE_MAXM)
    nsub = jnp.where(live, (nrows + MOE_SUB - 1) // MOE_SUB, 0)
    row0 = pstart[e_of_s] + k_in * MOE_MAXM
    total_sub = (pend[-1] // MOE_SUB).reshape(1)
    return (e_of_s.astype(jnp.int32), row0.astype(jnp.int32), nsub.astype(jnp.int32),
            total_sub.astype(jnp.int32), tok_sorted, dest.astype(jnp.int32))


def _moe_kernel(seg_e, seg_row0, seg_nsub, total_sub, tok_ref,
                h2_hbm, wg_ref, wl_ref, wd_ref, bg_ref, bl_ref, bd_ref,
                y_hbm,
                xbuf, ybuf, stage, wgb, wlb, wdb, gsem, osem):
    s = pl.program_id(0)
    f = pl.program_id(1)
    nsub = seg_nsub[s]
    row0 = seg_row0[s]
    sub = MOE_SUB

    def row_copy(slot, base, r):
        tok = tok_ref[base + r]
        return pltpu.make_async_copy(h2_hbm.at[pl.ds(tok, 1)], stage.at[slot, pl.ds(r, 1)],
                                     gsem.at[slot])

    def issue(chunk, slot):
        base = row0 + chunk * sub

        def body(r, carry):
            row_copy(slot, base, r).start()
            return carry
        lax.fori_loop(0, sub, body, 0, unroll=8)

    def wait(chunk, slot):
        base = row0 + chunk * sub

        def body(r, carry):
            row_copy(slot, base, r).wait()
            return carry
        lax.fori_loop(0, sub, body, 0, unroll=8)

    @pl.when(jnp.logical_and(f == 0, nsub > 0))
    def _gather():
        issue(0, 0)

        def chunk_body(c, carry):
            slot = c % 2

            @pl.when(c + 1 < nsub)
            def _():
                issue(c + 1, 1 - slot)
            wait(c, slot)
            xbuf[pl.ds(pl.multiple_of(c * sub, sub), sub), :] = stage[slot].astype(BF16)
            return carry
        lax.fori_loop(0, nsub, chunk_body, 0)

    @pl.when(nsub > 0)
    def _compute():
        wgb[...] = wg_ref[0].astype(BF16)
        wlb[...] = wl_ref[0].astype(BF16)
        wdb[...] = wd_ref[0].astype(BF16)

        def sub_body(i, carry):
            rows = pl.ds(pl.multiple_of(i * sub, sub), sub)
            xs = xbuf[rows, :]
            glu = jnp.dot(xs, wgb[...], preferred_element_type=F32) + bg_ref[0]
            lin = jnp.dot(xs, wlb[...], preferred_element_type=F32) + bl_ref[0]
            glu = jnp.minimum(glu, SWIGLU_LIMIT)
            lin = jnp.clip(lin, -SWIGLU_LIMIT, SWIGLU_LIMIT)
            act = (glu * jax.nn.sigmoid(SWIGLU_ALPHA * glu) * (lin + 1.0)).astype(BF16)
            yd = jnp.dot(act, wdb[...], preferred_element_type=F32)

            @pl.when(f == 0)
            def _():
                ybuf[rows, :] = yd + bd_ref[0]

            @pl.when(f > 0)
            def _():
                ybuf[rows, :] += yd
            return carry
        lax.fori_loop(0, nsub, sub_body, 0)

    def out_copy(src_rows, dst_row):
        return pltpu.make_async_copy(ybuf.at[pl.ds(src_rows, sub)],
                                     y_hbm.at[pl.ds(dst_row, sub)], osem.at[0])

    @pl.when(jnp.logical_and(f == MOE_NF - 1, nsub > 0))
    def _writeback():
        def start(i, carry):
            out_copy(pl.multiple_of(i * sub, sub), pl.multiple_of(row0 + i * sub, sub)).start()
            return carry
        lax.fori_loop(0, nsub, start, 0)

        def done(i, carry):
            out_copy(pl.multiple_of(i * sub, sub), pl.multiple_of(row0 + i * sub, sub)).wait()
            return carry
        lax.fori_loop(0, nsub, done, 0)

    @pl.when(jnp.logical_and(s == pl.num_programs(0) - 1, f == MOE_NF - 1))
    def _tail():
        n_tail = y_hbm.shape[0] // sub - total_sub[0]

        @pl.when(n_tail > 0)
        def _():
            ybuf[pl.ds(0, sub), :] = jnp.zeros((sub, D_MODEL), F32)

            def start(i, carry):
                out_copy(0, pl.multiple_of((total_sub[0] + i) * sub, sub)).start()
                return carry
            lax.fori_loop(0, n_tail, start, 0)

            def done(i, carry):
                out_copy(0, pl.multiple_of((total_sub[0] + i) * sub, sub)).wait()
                return carry
            lax.fori_loop(0, n_tail, done, 0)


def _moe(h2, seg_e, seg_row0, seg_nsub, total_sub, tok_sorted, w_gate_up, b_gate_up, w_down,
         b_down, cap):
    n_seg = seg_e.shape[0]
    last_f = MOE_NF - 1

    def fidx(s, f, nsub):
        return jnp.where(nsub[s] > 0, f, last_f)

    def wg_map(s, f, e, r0, nsub, ts, tok):
        return (e[s], 0, fidx(s, f, nsub))

    def wl_map(s, f, e, r0, nsub, ts, tok):
        return (e[s], 0, MOE_NF + fidx(s, f, nsub))

    def wd_map(s, f, e, r0, nsub, ts, tok):
        return (e[s], fidx(s, f, nsub), 0)

    def bd_map(s, f, e, r0, nsub, ts, tok):
        return (e[s], 0, 0)

    grid_spec = pltpu.PrefetchScalarGridSpec(
        num_scalar_prefetch=5,
        grid=(n_seg, MOE_NF),
        in_specs=[
            pl.BlockSpec(memory_space=pl.ANY),
            pl.BlockSpec((1, D_MODEL, MOE_F), wg_map),
            pl.BlockSpec((1, D_MODEL, MOE_F), wl_map),
            pl.BlockSpec((1, MOE_F, D_MODEL), wd_map),
            pl.BlockSpec((1, 1, MOE_F), wg_map),
            pl.BlockSpec((1, 1, MOE_F), wl_map),
            pl.BlockSpec((1, 1, D_MODEL), bd_map),
        ],
        out_specs=pl.BlockSpec(memory_space=pl.ANY),
        scratch_shapes=[
            pltpu.VMEM((MOE_MAXM, D_MODEL), BF16),
            pltpu.VMEM((MOE_MAXM, D_MODEL), F32),
            pltpu.VMEM((2, MOE_SUB, D_MODEL), F32),
            pltpu.VMEM((D_MODEL, MOE_F), BF16),
            pltpu.VMEM((D_MODEL, MOE_F), BF16),
            pltpu.VMEM((MOE_F, D_MODEL), BF16),
            pltpu.SemaphoreType.DMA((2,)),
            pltpu.SemaphoreType.DMA((1,)),
        ],
    )
    return pl.pallas_call(
        _moe_kernel,
        grid_spec=grid_spec,
        out_shape=jax.ShapeDtypeStruct((cap, D_MODEL), F32),
        compiler_params=_cparams(("arbitrary", "arbitrary")),
        name="moe",
    )(seg_e, seg_row0, seg_nsub, total_sub, tok_sorted,
      h2, w_gate_up, w_gate_up, w_down,
      b_gate_up.reshape(N_EXPERTS, 1, 2 * D_FF), b_gate_up.reshape(N_EXPERTS, 1, 2 * D_FF),
      b_down.reshape(N_EXPERTS, 1, D_MODEL))


def _combine_kernel(pos_ref, y_hbm, x1_ref, gate_ref, gfin_ref, op_ref, os_ref, buf, sem,
                    *, n_prompt_tiles):
    i = pl.program_id(0)
    n = pl.num_programs(0)
    tm = ROW_TM

    def row_copy(tile, slot, r, k):
        p = pos_ref[(tile * tm + r) * TOP_K + k]
        return pltpu.make_async_copy(y_hbm.at[pl.ds(p, 1)], buf.at[slot, k, pl.ds(r, 1)],
                                     sem.at[slot])

    def issue(tile, slot):
        def body(r, carry):
            for k in range(TOP_K):
                row_copy(tile, slot, r, k).start()
            return carry
        lax.fori_loop(0, tm, body, 0, unroll=4)

    def wait(tile, slot):
        def body(r, carry):
            for k in range(TOP_K):
                row_copy(tile, slot, r, k).wait()
            return carry
        lax.fori_loop(0, tm, body, 0, unroll=4)

    slot = i % 2

    @pl.when(i == 0)
    def _():
        issue(0, 0)

    @pl.when(i + 1 < n)
    def _():
        issue(i + 1, 1 - slot)

    wait(i, slot)
    gate = gate_ref[...]
    x2 = x1_ref[...]
    for k in range(TOP_K):
        x2 = x2 + gate[:, k:k + 1] * buf[slot, k]
    ms = jnp.mean(x2 * x2, axis=-1, keepdims=True)
    y = x2 * lax.rsqrt(ms + EPS) * gfin_ref[...]

    @pl.when(i < n_prompt_tiles)
    def _():
        op_ref[...] = y

    @pl.when(i >= n_prompt_tiles)
    def _():
        os_ref[...] = y


def _combine(pos, y_sorted, x1, gate, g_final, n_prompt, n_sample):
    tm = ROW_TM
    n_tok = x1.shape[0]
    n_prompt_tiles = n_prompt // tm
    n_sample_tiles = n_sample // tm

    grid_spec = pltpu.PrefetchScalarGridSpec(
        num_scalar_prefetch=1,
        grid=(n_tok // tm,),
        in_specs=[
            pl.BlockSpec(memory_space=pl.ANY),
            pl.BlockSpec((tm, D_MODEL), lambda i, pos: (i, 0)),
            pl.BlockSpec((tm, TOP_K), lambda i, pos: (i, 0)),
            pl.BlockSpec((1, D_MODEL), lambda i, pos: (0, 0)),
        ],
        out_specs=[
            pl.BlockSpec((tm, D_MODEL), lambda i, pos: (jnp.minimum(i, n_prompt_tiles - 1), 0)),
            pl.BlockSpec((tm, D_MODEL),
                         lambda i, pos: (jnp.clip(i - n_prompt_tiles, 0, n_sample_tiles - 1), 0)),
        ],
        scratch_shapes=[
            pltpu.VMEM((2, TOP_K, tm, D_MODEL), F32),
            pltpu.SemaphoreType.DMA((2,)),
        ],
    )
    return pl.pallas_call(
        functools.partial(_combine_kernel, n_prompt_tiles=n_prompt_tiles),
        grid_spec=grid_spec,
        out_shape=[
            jax.ShapeDtypeStruct((n_prompt, D_MODEL), F32),
            jax.ShapeDtypeStruct((n_sample, D_MODEL), F32),
        ],
        compiler_params=_cparams(("arbitrary",)),
        name="combine",
    )(pos, y_sorted, x1, gate, g_final.reshape(1, D_MODEL))


def kernel(x_prompt, x_sample, cache_k, cache_v, state_pool, meta_tokens, g_mix, w_in, sinks,
           w_pool_mix, pool_scale, w_br_attn, w_br_pool, w_out, g_ffn, w_router, b_router,
           w_gate_up, b_gate_up, w_down, b_down, g_final):
    depth = w_in.shape[0]
    assert depth == 1, "single-layer step only"
    batch, seq, _ = x_prompt.shape
    dec_batch, dec_seq, _ = x_sample.shape
    n_prompt = batch * seq
    n_sample = dec_batch * dec_seq
    n_tok = n_prompt + n_sample
    assert seq % ATTN_TILE == 0 and n_prompt % ROW_TM == 0 and n_sample % ROW_TM == 0
    assert dec_seq == SUBLANES

    n_rows = -(-(n_tok + ATTN_TILE) // PROJ_TM) * PROJ_TM
    pad = jnp.zeros((n_rows - n_tok - N_META, D_MODEL), F32)
    x_all = jnp.concatenate([x_prompt.reshape(n_prompt, D_MODEL), x_sample.reshape(n_sample, D_MODEL),
                             pad, meta_tokens.astype(F32)], axis=0)

    p_all = _inproj(x_all, g_mix[0], w_in[0])

    ck = cache_k[0].reshape(dec_batch, WINDOW, KV_WIDTH)
    cv = cache_v[0].reshape(dec_batch, WINDOW, KV_WIDTH)
    state_pad = jnp.pad(state_pool[0], ((0, 0), (N_META - POOL_BUF, 0), (0, 0)))
    oa_p, z_p = _attn_prompt(p_all, sinks[0], batch, seq)
    oa_s, z_s = _attn_sample(p_all, sinks[0], ck, cv, state_pad, n_prompt, dec_batch, dec_seq)
    o_attn = jnp.concatenate([oa_p, oa_s], axis=0)
    z = jnp.concatenate([z_p, z_s], axis=0)

    x1, h2, top_idx, gate = _finish(
        x_all, o_attn, z, p_all, w_pool_mix[0].astype(BF16), pool_scale[0],
        w_br_attn[0].astype(BF16), w_br_pool[0].astype(BF16), w_out[0].astype(BF16),
        g_ffn[0], w_router[0], b_router[0])

    n_assign = n_tok * TOP_K
    n_seg = N_EXPERTS + n_assign // MOE_MAXM
    cap = (n_assign // MOE_SUB + N_EXPERTS) * MOE_SUB
    seg_e, seg_row0, seg_nsub, total_sub, tok_sorted, dest = _route(top_idx, n_seg, cap)
    y_sorted = _moe(h2, seg_e, seg_row0, seg_nsub, total_sub, tok_sorted,
                    w_gate_up[0], b_gate_up[0], w_down[0], b_down[0], cap)
    y_p, y_s = _combine(dest, y_sorted, x1, gate, g_final, n_prompt, n_sample)

    k_cols = slice(ATTN_WIDTH, ATTN_WIDTH + KV_WIDTH)
    v_cols = slice(ATTN_WIDTH + KV_WIDTH, ATTN_WIDTH + 2 * KV_WIDTH)
    u_cols = slice(ATTN_WIDTH + 2 * KV_WIDTH, ATTN_WIDTH + 2 * KV_WIDTH + POOL_WIDTH)
    pp = p_all[:n_prompt].reshape(batch, seq, IN_COLS)
    ps = p_all[n_prompt:n_tok].reshape(dec_batch, dec_seq, IN_COLS)
    new_k_p = pp[:, seq - WINDOW:, k_cols].reshape(1, batch, WINDOW, N_KV_HEADS, HEAD_DIM)
    new_v_p = pp[:, seq - WINDOW:, v_cols].reshape(1, batch, WINDOW, N_KV_HEADS, HEAD_DIM)
    new_u_p = pp[:, seq - POOL_BUF:, u_cols][None]
    new_k_s = jnp.concatenate([ck[:, dec_seq:], ps[:, :, k_cols]], axis=1).reshape(
        1, dec_batch, WINDOW, N_KV_HEADS, HEAD_DIM)
    new_v_s = jnp.concatenate([cv[:, dec_seq:], ps[:, :, v_cols]], axis=1).reshape(
        1, dec_batch, WINDOW, N_KV_HEADS, HEAD_DIM)
    new_u_s = jnp.concatenate([state_pool[0][:, dec_seq:], ps[:, :, u_cols]], axis=1)[None]

    return (y_p.reshape(batch, seq, D_MODEL), y_s.reshape(dec_batch, dec_seq, D_MODEL),
            new_k_p, new_v_p, new_u_p, new_k_s, new_v_s, new_u_s)
```

```python
import functools

import jax
import jax.numpy as jnp
from jax import lax
from jax.experimental import pallas as pl
from jax.experimental.pallas import tpu as pltpu

F32 = jnp.float32
BF16 = jnp.bfloat16

D_MODEL = 2048
N_META = 16
N_HEADS = 32
N_KV_HEADS = 8
HEAD_DIM = 64
GROUP = N_HEADS // N_KV_HEADS
WINDOW = 128
ATTN_WIDTH = N_HEADS * HEAD_DIM
KV_WIDTH = N_KV_HEADS * HEAD_DIM
POOL_WIDTH = D_MODEL // 2
POOL_WINDOWS = (2, 4, 8, 16)
POOL_GROUP_DIM = POOL_WIDTH // len(POOL_WINDOWS)
POOL_BUF = max(POOL_WINDOWS) - 1
N_EXPERTS = 32
TOP_K = 4
D_FF = D_MODEL
SWIGLU_LIMIT = 7.0
SWIGLU_ALPHA = 1.702
EPS = 1e-5
IN_COLS = ATTN_WIDTH + 2 * KV_WIDTH + POOL_WIDTH + 2 * D_MODEL

LANES = 128
SUBLANES = 8
VMEM_LIMIT_BYTES = 56 * 1024 * 1024

ATTN_TILE = WINDOW
PROJ_TM = 512
PROJ_TN = 1024
ROW_TM = 256
SAMPLE_SEQS = 4
MOE_SUB = 256
MOE_MAXM = 6 * MOE_SUB
MOE_FA = 512
MOE_FB = 256
MOE_NFA = D_FF // MOE_FA
MOE_NFB = D_MODEL // MOE_FB
NEG_BIG = -1e30

_SLOPES = tuple(float(2.0 ** (-8.0 * (i + 1) / N_HEADS)) for i in range(N_HEADS))


def _cparams(sem, vmem=VMEM_LIMIT_BYTES):
    return pltpu.CompilerParams(dimension_semantics=sem, vmem_limit_bytes=vmem)


def _inproj_kernel(x_ref, g_ref, w_ref, o_ref, wbf_ref):
    @pl.when(pl.program_id(1) == 0)
    def _():
        wbf_ref[...] = w_ref[...].astype(BF16)

    x = x_ref[...]
    ms = jnp.mean(x * x, axis=-1, keepdims=True)
    h = (x * lax.rsqrt(ms + EPS) * g_ref[...]).astype(BF16)
    o_ref[...] = jnp.dot(h, wbf_ref[...], preferred_element_type=F32)


def _inproj(x_all, g_mix, w_in):
    n_rows = x_all.shape[0]
    grid = (IN_COLS // PROJ_TN, n_rows // PROJ_TM)
    return pl.pallas_call(
        _inproj_kernel,
        grid=grid,
        in_specs=[
            pl.BlockSpec((PROJ_TM, D_MODEL), lambda n, m: (m, 0)),
            pl.BlockSpec((1, D_MODEL), lambda n, m: (0, 0)),
            pl.BlockSpec((D_MODEL, PROJ_TN), lambda n, m: (0, n)),
        ],
        out_specs=pl.BlockSpec((PROJ_TM, PROJ_TN), lambda n, m: (m, n)),
        out_shape=jax.ShapeDtypeStruct((n_rows, IN_COLS), F32),
        scratch_shapes=[pltpu.VMEM((D_MODEL, PROJ_TN), BF16)],
        compiler_params=_cparams(("arbitrary", "arbitrary")),
        name="inproj",
    )(x_all, g_mix.reshape(1, D_MODEL), w_in)


def _attn_bias(n_q, n_keys, first_key):
    r = jnp.arange(n_q, dtype=jnp.int32)[:, None]
    c = jnp.arange(n_keys, dtype=jnp.int32)[None, :]
    dist = r + WINDOW - c
    valid = (dist >= 0) & (dist <= WINDOW) & (c >= first_key)
    slopes = jnp.asarray(_SLOPES, F32).reshape(N_KV_HEADS, GROUP, 1, 1)
    bias = jnp.where(valid[None, None], -slopes * dist.astype(F32)[None, None], NEG_BIG)
    return bias.reshape(N_KV_HEADS, GROUP * n_q, n_keys)


def _group_sinks(sink_ref, kvh, n_q):
    g_row = lax.broadcasted_iota(jnp.int32, (GROUP * n_q, 1), 0) // n_q
    sink = jnp.zeros((GROUP * n_q, 1), F32)
    for g in range(GROUP):
        sink = jnp.where(g_row == g, sink_ref[kvh * GROUP + g], sink)
    return sink


def _group_attention(q, k, v, bias, sink_ref, kvh, n_q):
    kh = k[:, kvh * HEAD_DIM:(kvh + 1) * HEAD_DIM]
    vh = v[:, kvh * HEAD_DIM:(kvh + 1) * HEAD_DIM]
    heads = [kvh * GROUP + g for g in range(GROUP)]
    qg = jnp.concatenate([q[:, hd * HEAD_DIM:(hd + 1) * HEAD_DIM] for hd in heads], axis=0)
    s = lax.dot_general(qg, kh, (((1,), (1,)), ((), ())), preferred_element_type=F32) + bias
    sink = _group_sinks(sink_ref, kvh, n_q)
    m = jnp.maximum(jnp.max(s, axis=-1, keepdims=True), sink)
    p = jnp.exp(s - m)
    den = jnp.sum(p, axis=-1, keepdims=True) + jnp.exp(sink - m)
    o = jnp.dot(p.astype(BF16), vh, preferred_element_type=F32)
    return o / den


def _pool_features(ext, n_halo):
    outs = []
    for g, w in enumerate(POOL_WINDOWS):
        e = ext[:, g * POOL_GROUP_DIM:(g + 1) * POOL_GROUP_DIM]
        s = e
        shift = 1
        while shift < w:
            s = s + pltpu.roll(s, shift, axis=0)
            shift *= 2
        outs.append(s[n_halo:] * (1.0 / w) - e[n_halo:])
    return jnp.concatenate(outs, axis=1)


def _attn_prompt_kernel(sink_ref, bias_ref, q_ref, ko_ref, kp_ref, vo_ref, vp_ref, uo_ref, up_ref,
                        o_ref, z_ref):
    t = ATTN_TILE
    q = (q_ref[...] * (HEAD_DIM ** -0.5)).astype(BF16)
    k = jnp.concatenate([kp_ref[...], ko_ref[...]], axis=0).astype(BF16)
    v = jnp.concatenate([vp_ref[...], vo_ref[...]], axis=0).astype(BF16)
    for kvh in range(N_KV_HEADS):
        o = _group_attention(q, k, v, bias_ref[0, kvh], sink_ref, kvh, t)
        for g in range(GROUP):
            hd = kvh * GROUP + g
            o_ref[:, hd * HEAD_DIM:(hd + 1) * HEAD_DIM] = o[g * t:(g + 1) * t].astype(o_ref.dtype)

    ext = jnp.concatenate([up_ref[...], uo_ref[...]], axis=0)
    z_ref[...] = _pool_features(ext, N_META).astype(z_ref.dtype)


def _attn_prompt(p_all, sinks, batch, seq):
    t = ATTN_TILE
    tiles = seq // t
    meta_blk = p_all.shape[0] // t - 1
    q_w, kv_w, u_w = ATTN_WIDTH, KV_WIDTH, POOL_WIDTH
    kcol, vcol, ucol = ATTN_WIDTH // kv_w, ATTN_WIDTH // kv_w + 1, (ATTN_WIDTH + 2 * kv_w) // u_w
    halo_per_tile = t // N_META
    bias = jnp.stack([_attn_bias(t, 2 * t, t - N_META), _attn_bias(t, 2 * t, 0)])

    def own(b, j):
        return b * tiles + j

    def prev(b, j):
        return jnp.where(j > 0, b * tiles + j - 1, meta_blk)

    return pl.pallas_call(
        _attn_prompt_kernel,
        grid=(batch, tiles),
        in_specs=[
            pl.BlockSpec(memory_space=pltpu.SMEM),
            pl.BlockSpec((1, N_KV_HEADS, GROUP * t, 2 * t), lambda b, j: (jnp.minimum(j, 1), 0, 0, 0)),
            pl.BlockSpec((t, q_w), lambda b, j: (own(b, j), 0)),
            pl.BlockSpec((t, kv_w), lambda b, j: (own(b, j), kcol)),
            pl.BlockSpec((t, kv_w), lambda b, j: (prev(b, j), kcol)),
            pl.BlockSpec((t, kv_w), lambda b, j: (own(b, j), vcol)),
            pl.BlockSpec((t, kv_w), lambda b, j: (prev(b, j), vcol)),
            pl.BlockSpec((t, u_w), lambda b, j: (own(b, j), ucol)),
            pl.BlockSpec((N_META, u_w), lambda b, j: (prev(b, j) * halo_per_tile + halo_per_tile - 1, ucol)),
        ],
        out_specs=[
            pl.BlockSpec((t, q_w), lambda b, j: (own(b, j), 0)),
            pl.BlockSpec((t, u_w), lambda b, j: (own(b, j), 0)),
        ],
        out_shape=[
            jax.ShapeDtypeStruct((batch * seq, q_w), BF16),
            jax.ShapeDtypeStruct((batch * seq, u_w), BF16),
        ],
        compiler_params=_cparams(("arbitrary", "arbitrary")),
        name="attn_prompt",
    )(sinks, bias, p_all, p_all, p_all, p_all, p_all, p_all, p_all)


def _attn_sample_kernel(sink_ref, bias_ref, q_ref, kn_ref, vn_ref, un_ref, ck_ref, cv_ref, sp_ref,
                        o_ref, z_ref):
    n_seqs = ck_ref.shape[0]
    n_new = q_ref.shape[0] // n_seqs
    q_all = (q_ref[...] * (HEAD_DIM ** -0.5)).astype(BF16)
    for si in range(n_seqs):
        rows = slice(si * n_new, (si + 1) * n_new)
        q = q_all[rows]
        k = jnp.concatenate([ck_ref[si], kn_ref[rows, :]], axis=0).astype(BF16)
        v = jnp.concatenate([cv_ref[si], vn_ref[rows, :]], axis=0).astype(BF16)
        for kvh in range(N_KV_HEADS):
            o = _group_attention(q, k, v, bias_ref[kvh], sink_ref, kvh, n_new)
            for g in range(GROUP):
                hd = kvh * GROUP + g
                o_ref[rows, hd * HEAD_DIM:(hd + 1) * HEAD_DIM] = (
                    o[g * n_new:(g + 1) * n_new].astype(o_ref.dtype))
        ext = jnp.concatenate([sp_ref[si], un_ref[rows, :]], axis=0)
        z_ref[rows, :] = _pool_features(ext, sp_ref.shape[1]).astype(z_ref.dtype)


def _attn_sample(p_all, sinks, cache_k, cache_v, state_pad, row0, dec_batch, dec_seq):
    q_w, kv_w, u_w = ATTN_WIDTH, KV_WIDTH, POOL_WIDTH
    kcol, vcol, ucol = ATTN_WIDTH // kv_w, ATTN_WIDTH // kv_w + 1, (ATTN_WIDTH + 2 * kv_w) // u_w
    ns = SAMPLE_SEQS
    rows = ns * dec_seq
    blk0 = row0 // rows
    n_halo = state_pad.shape[1]
    n_keys = WINDOW + dec_seq
    bias = _attn_bias(dec_seq, n_keys, 0)
    return pl.pallas_call(
        _attn_sample_kernel,
        grid=(dec_batch // ns,),
        in_specs=[
            pl.BlockSpec(memory_space=pltpu.SMEM),
            pl.BlockSpec((N_KV_HEADS, GROUP * dec_seq, n_keys), lambda b: (0, 0, 0)),
            pl.BlockSpec((rows, q_w), lambda b: (blk0 + b, 0)),
            pl.BlockSpec((rows, kv_w), lambda b: (blk0 + b, kcol)),
            pl.BlockSpec((rows, kv_w), lambda b: (blk0 + b, vcol)),
            pl.BlockSpec((rows, u_w), lambda b: (blk0 + b, ucol)),
            pl.BlockSpec((ns, WINDOW, kv_w), lambda b: (b, 0, 0)),
            pl.BlockSpec((ns, WINDOW, kv_w), lambda b: (b, 0, 0)),
            pl.BlockSpec((ns, n_halo, u_w), lambda b: (b, 0, 0)),
        ],
        out_specs=[
            pl.BlockSpec((rows, q_w), lambda b: (b, 0)),
            pl.BlockSpec((rows, u_w), lambda b: (b, 0)),
        ],
        out_shape=[
            jax.ShapeDtypeStruct((dec_batch * dec_seq, q_w), BF16),
            jax.ShapeDtypeStruct((dec_batch * dec_seq, u_w), BF16),
        ],
        compiler_params=_cparams(("arbitrary",)),
        name="attn_sample",
    )(sinks, bias, p_all, p_all, p_all, p_all, cache_k, cache_v, state_pad)


def _finish_kernel(x_ref, oa_ref, z_ref, ag_ref, pg_ref, wpm_ref, ps_ref, wba_ref, wbp_ref,
                   wo_ref, gf_ref, wr_ref, br_ref,
                   x1_ref, h2_ref, idx_ref, gate_ref, rank_ref, cnt_ref, carry_ref):
    tm = x_ref.shape[0]

    @pl.when(pl.program_id(0) == 0)
    def _():
        carry_ref[...] = jnp.zeros_like(carry_ref)

    z = z_ref[...]
    zp = jnp.concatenate(
        [jnp.dot(z[:, g * POOL_GROUP_DIM:(g + 1) * POOL_GROUP_DIM], wpm_ref[g],
                 preferred_element_type=F32) for g in range(len(POOL_WINDOWS))], axis=1)
    zp = (zp * ps_ref[...]).astype(BF16)
    ya = jnp.dot(oa_ref[...], wba_ref[...], preferred_element_type=F32)
    yp = jnp.dot(zp, wbp_ref[...], preferred_element_type=F32)
    merged = jax.nn.sigmoid(ag_ref[...]) * ya + jax.nn.sigmoid(pg_ref[...]) * yp
    x1 = x_ref[...] + jnp.dot(merged.astype(BF16), wo_ref[...], preferred_element_type=F32)
    x1_ref[...] = x1
    ms = jnp.mean(x1 * x1, axis=-1, keepdims=True)
    h2 = x1 * lax.rsqrt(ms + EPS) * gf_ref[...]
    h2_ref[...] = h2

    h_hi = h2.astype(BF16)
    h_lo = (h2 - h_hi.astype(F32)).astype(BF16)
    t = jnp.dot(h_hi, wr_ref[...], preferred_element_type=F32)
    logits = (t[:, :N_EXPERTS] + t[:, N_EXPERTS:]
              + jnp.dot(h_lo, wr_ref[:, :N_EXPERTS], preferred_element_type=F32) + br_ref[...])

    col = lax.broadcasted_iota(jnp.int32, logits.shape, 1).astype(F32)
    vals, idxs = [], []
    for _ in range(TOP_K):
        m = jnp.max(logits, axis=-1, keepdims=True)
        idx = jnp.min(jnp.where(logits == m, col, float(N_EXPERTS)), axis=-1, keepdims=True)
        vals.append(m)
        idxs.append(idx)
        logits = jnp.where(col == idx, -jnp.inf, logits)
    exps = [jnp.exp(v - vals[0]) for v in vals]
    den = exps[0] + exps[1] + exps[2] + exps[3]

    member = jnp.zeros(logits.shape, F32)
    for k in range(TOP_K):
        member = member + (col == idxs[k]).astype(F32)
    ri = lax.broadcasted_iota(jnp.int32, (tm, tm), 0)
    ci = lax.broadcasted_iota(jnp.int32, (tm, tm), 1)
    earlier = (ri > ci).astype(BF16)
    before = jnp.dot(earlier, member.astype(BF16), preferred_element_type=F32) + carry_ref[...]
    for k in range(TOP_K):
        idx_ref[:, k:k + 1] = idxs[k].astype(jnp.int32)
        gate_ref[:, k:k + 1] = exps[k] / den
        rank_ref[:, k:k + 1] = jnp.sum(jnp.where(col == idxs[k], before, 0.0), axis=-1,
                                       keepdims=True).astype(jnp.int32)
    carry_ref[...] += jnp.sum(member, axis=0, keepdims=True)
    cnt_ref[...] = carry_ref[...]


def _finish(x_all, o_attn, z, p_all, wpm, pool_scale, wba, wbp, wo, g_ffn, w_router2, b_router):
    n_tok = o_attn.shape[0]
    tm = ROW_TM
    acol, pcol = (IN_COLS - 2 * D_MODEL) // D_MODEL, (IN_COLS - D_MODEL) // D_MODEL
    const = pl.Buffered(1)

    def whole(shape):
        nd = len(shape)
        return pl.BlockSpec(shape, lambda i: (0,) * nd, pipeline_mode=const)

    return pl.pallas_call(
        _finish_kernel,
        grid=(n_tok // tm,),
        in_specs=[
            pl.BlockSpec((tm, D_MODEL), lambda i: (i, 0)),
            pl.BlockSpec((tm, ATTN_WIDTH), lambda i: (i, 0)),
            pl.BlockSpec((tm, POOL_WIDTH), lambda i: (i, 0)),
            pl.BlockSpec((tm, D_MODEL), lambda i: (i, acol)),
            pl.BlockSpec((tm, D_MODEL), lambda i: (i, pcol)),
            whole(wpm.shape), whole((1, POOL_WIDTH)), whole(wba.shape), whole(wbp.shape),
            whole(wo.shape), whole((1, D_MODEL)), whole(w_router2.shape), whole((1, N_EXPERTS)),
        ],
        out_specs=[
            pl.BlockSpec((tm, D_MODEL), lambda i: (i, 0)),
            pl.BlockSpec((tm, D_MODEL), lambda i: (i, 0)),
            pl.BlockSpec((tm, TOP_K), lambda i: (i, 0)),
            pl.BlockSpec((tm, TOP_K), lambda i: (i, 0)),
            pl.BlockSpec((tm, TOP_K), lambda i: (i, 0)),
            pl.BlockSpec((1, N_EXPERTS), lambda i: (0, 0)),
        ],
        out_shape=[
            jax.ShapeDtypeStruct((n_tok, D_MODEL), F32),
            jax.ShapeDtypeStruct((n_tok, D_MODEL), F32),
            jax.ShapeDtypeStruct((n_tok, TOP_K), jnp.int32),
            jax.ShapeDtypeStruct((n_tok, TOP_K), F32),
            jax.ShapeDtypeStruct((n_tok, TOP_K), jnp.int32),
            jax.ShapeDtypeStruct((1, N_EXPERTS), F32),
        ],
        scratch_shapes=[pltpu.VMEM((1, N_EXPERTS), F32)],
        compiler_params=_cparams(("arbitrary",)),
        name="finish",
    )(x_all, o_attn, z, p_all, p_all, wpm, pool_scale.reshape(1, POOL_WIDTH), wba, wbp, wo,
      g_ffn.reshape(1, D_MODEL), w_router2, b_router.reshape(1, N_EXPERTS))


def _route(top_idx, rank, counts_f, n_seg, cap):
    n_assign = top_idx.size
    counts = counts_f.reshape(N_EXPERTS).astype(jnp.int32)
    padded = (counts + MOE_SUB - 1) // MOE_SUB * MOE_SUB
    pend = jnp.cumsum(padded)
    pstart = pend - padded
    experts = jnp.arange(N_EXPERTS, dtype=jnp.int32)
    start_of = jnp.sum(jnp.where(top_idx[..., None] == experts, pstart, 0), axis=-1)
    dest = (start_of + rank).reshape(-1)
    tok_sorted = jnp.zeros((cap,), jnp.int32).at[dest].set(
        jnp.arange(n_assign, dtype=jnp.int32) // TOP_K)

    nseg_e = (counts + MOE_MAXM - 1) // MOE_MAXM
    seg_end = jnp.cumsum(nseg_e)
    seg_base = seg_end - nseg_e
    s_ids = jnp.arange(n_seg, dtype=jnp.int32)
    live = s_ids < seg_end[-1]
    last = jnp.maximum(seg_end[-1] - 1, 0)
    s_eff = jnp.where(live, s_ids, last)
    e_of_s = jnp.minimum(jnp.sum(s_eff[:, None] >= seg_end[None, :], axis=1), N_EXPERTS - 1)
    e_prev = jnp.concatenate([e_of_s[:1], e_of_s[:-1]])
    k_in = s_eff - seg_base[e_of_s]
    nrows = jnp.clip(counts[e_of_s] - k_in * MOE_MAXM, 0, MOE_MAXM)
    nsub = jnp.where(live, (nrows + MOE_SUB - 1) // MOE_SUB, 0)
    row0 = pstart[e_of_s] + k_in * MOE_MAXM
    total_sub = (pend[-1] // MOE_SUB).reshape(1)
    i32 = jnp.int32
    return (e_of_s.astype(i32), e_prev.astype(i32), row0.astype(i32), nsub.astype(i32),
            total_sub.astype(i32), tok_sorted, dest.astype(i32))


def _for_pairs(n, fn):
    def pair(p, carry):
        fn(2 * p)
        fn(2 * p + 1)
        return carry
    lax.fori_loop(0, n // 2, pair, 0)

    @pl.when(n % 2 == 1)
    def _():
        fn(n - 1)


def _moe_kernel(seg_e, seg_ep, seg_row0, seg_nsub, total_sub, tok_ref,
                h2_hbm, wg_ref, wl_ref, wd_ref, bg_ref, bl_ref, bd_ref,
                y_hbm,
                xbuf, actbuf, stage, wab, wdb, ostage, pend, gsem, osem):
    s = pl.program_id(0)
    j = pl.program_id(1)
    nsub = seg_nsub[s]
    row0 = seg_row0[s]
    live = nsub > 0
    sub = MOE_SUB
    fa, fb = MOE_FA, MOE_FB

    @pl.when(jnp.logical_and(s == 0, j == 0))
    def _init():
        pend[0] = 0
        pend[1] = 0

    def issue(chunk, slot):
        base = row0 + chunk * sub

        def body(r, carry):
            tok = tok_ref[base + r]
            pltpu.make_async_copy(h2_hbm.at[pl.ds(tok, 1)], stage.at[slot, pl.ds(r, 1)],
                                  gsem.at[slot]).start()
            return carry
        lax.fori_loop(0, sub, body, 0, unroll=8)

    def wait_chunk(slot):
        pltpu.make_async_copy(h2_hbm.at[pl.ds(0, sub)], stage.at[slot], gsem.at[slot]).wait()

    @pl.when(jnp.logical_and(j == 0, live))
    def _gather():
        issue(0, 0)

        def chunk_body(c, carry):
            slot = c % 2

            @pl.when(c + 1 < nsub)
            def _():
                issue(c + 1, 1 - slot)
            wait_chunk(slot)
            xbuf[pl.ds(pl.multiple_of(c * sub, sub), sub), :] = stage[slot].astype(BF16)
            return carry
        lax.fori_loop(0, nsub, chunk_body, 0)

    @pl.when(jnp.logical_and(j < MOE_NFA, live))
    def _up():
        wab[:, :fa] = wg_ref[0].astype(BF16)
        wab[:, fa:] = wl_ref[0].astype(BF16)
        bias = jnp.concatenate([bg_ref[0], bl_ref[0]], axis=1)

        def one(i):
            rows = pl.ds(pl.multiple_of(i * sub, sub), sub)
            gu = jnp.dot(xbuf[rows, :], wab[...], preferred_element_type=F32) + bias
            glu = jnp.minimum(gu[:, :fa], SWIGLU_LIMIT)
            lin = jnp.clip(gu[:, fa:], -SWIGLU_LIMIT, SWIGLU_LIMIT)
            actbuf[j, rows, :] = (glu * jax.nn.sigmoid(SWIGLU_ALPHA * glu) * (lin + 1.0)).astype(BF16)
        _for_pairs(nsub, one)

    def out_wait(slot):
        def body(i, carry):
            pltpu.make_async_copy(ostage.at[slot, pl.ds(0, sub)],
                                  y_hbm.at[pl.ds(0, sub), pl.ds(0, fb)], osem.at[slot]).wait()
            return carry
        lax.fori_loop(0, pend[slot], body, 0)
        pend[slot] = 0

    @pl.when(jnp.logical_and(j >= MOE_NFA, live))
    def _down():
        jb = j - MOE_NFA
        slot = jb % 2
        col0 = pl.multiple_of(jb * fb, fb)
        wdb[...] = wd_ref[0].astype(BF16)
        out_wait(slot)

        def one(i):
            r = pl.multiple_of(i * sub, sub)
            rows = pl.ds(r, sub)
            act = jnp.concatenate([actbuf[c, rows, :] for c in range(MOE_NFA)], axis=1)
            ostage[slot, rows, :] = jnp.dot(act, wdb[...], preferred_element_type=F32) + bd_ref[0]
            pltpu.make_async_copy(
                ostage.at[slot, rows],
                y_hbm.at[pl.ds(pl.multiple_of(row0 + r, sub), sub), pl.ds(col0, fb)],
                osem.at[slot]).start()
        _for_pairs(nsub, one)
        pend[slot] = nsub

    @pl.when(jnp.logical_and(s == pl.num_programs(0) - 1, j == pl.num_programs(1) - 1))
    def _final():
        out_wait(0)
        out_wait(1)
        n_tail = y_hbm.shape[0] // sub - total_sub[0]

        @pl.when(n_tail > 0)
        def _():
            ostage[0, pl.ds(0, sub), :] = jnp.zeros((sub, fb), F32)

            def fill(t, carry):
                r = pl.multiple_of((total_sub[0] + t) * sub, sub)
                for cb in range(MOE_NFB):
                    pltpu.make_async_copy(ostage.at[0, pl.ds(0, sub)],
                                          y_hbm.at[pl.ds(r, sub), pl.ds(cb * fb, fb)],
                                          osem.at[0]).start()
                return carry
            lax.fori_loop(0, n_tail, fill, 0)
            pend[0] = n_tail * MOE_NFB
            out_wait(0)


def _moe(h2, seg_e, seg_ep, seg_row0, seg_nsub, total_sub, tok_sorted, w_gate_up, b_gate_up,
         w_down, b_down, cap):
    n_seg = seg_e.shape[0]
    nfa, nfb = MOE_NFA, MOE_NFB

    def up_chunk(s, j, nsub):
        return jnp.where(nsub[s] > 0, jnp.minimum(j, nfa - 1), nfa - 1)

    def wg_map(s, j, e, ep, r0, nsub, ts, tok):
        return (e[s], 0, up_chunk(s, j, nsub))

    def wl_map(s, j, e, ep, r0, nsub, ts, tok):
        return (e[s], 0, nfa + up_chunk(s, j, nsub))

    def wd_map(s, j, e, ep, r0, nsub, ts, tok):
        in_up = j < nfa
        eb = jnp.where(in_up, ep[s], e[s])
        jb = jnp.where(in_up, jnp.where(s > 0, nfb - 1, 0),
                       jnp.where(nsub[s] > 0, j - nfa, nfb - 1))
        return (eb, 0, jb)

    grid_spec = pltpu.PrefetchScalarGridSpec(
        num_scalar_prefetch=6,
        grid=(n_seg, nfa + nfb),
        in_specs=[
            pl.BlockSpec(memory_space=pl.ANY),
            pl.BlockSpec((1, D_MODEL, MOE_FA), wg_map),
            pl.BlockSpec((1, D_MODEL, MOE_FA), wl_map),
            pl.BlockSpec((1, D_FF, MOE_FB), wd_map),
            pl.BlockSpec((1, 1, MOE_FA), wg_map),
            pl.BlockSpec((1, 1, MOE_FA), wl_map),
            pl.BlockSpec((1, 1, MOE_FB), wd_map),
        ],
        out_specs=pl.BlockSpec(memory_space=pl.ANY),
        scratch_shapes=[
            pltpu.VMEM((MOE_MAXM, D_MODEL), BF16),
            pltpu.VMEM((nfa, MOE_MAXM, MOE_FA), BF16),
            pltpu.VMEM((2, MOE_SUB, D_MODEL), F32),
            pltpu.VMEM((D_MODEL, 2 * MOE_FA), BF16),
            pltpu.VMEM((D_FF, MOE_FB), BF16),
            pltpu.VMEM((2, MOE_MAXM, MOE_FB), F32),
            pltpu.SMEM((2,), jnp.int32),
            pltpu.SemaphoreType.DMA((2,)),
            pltpu.SemaphoreType.DMA((2,)),
        ],
    )
    return pl.pallas_call(
        _moe_kernel,
        grid_spec=grid_spec,
        out_shape=jax.ShapeDtypeStruct((cap, D_MODEL), F32),
        compiler_params=_cparams(("arbitrary", "arbitrary")),
        name="moe",
    )(seg_e, seg_ep, seg_row0, seg_nsub, total_sub, tok_sorted,
      h2, w_gate_up, w_gate_up, w_down,
      b_gate_up.reshape(N_EXPERTS, 1, 2 * D_FF), b_gate_up.reshape(N_EXPERTS, 1, 2 * D_FF),
      b_down.reshape(N_EXPERTS, 1, D_MODEL))


def _combine_kernel(pos_ref, y_hbm, x1_ref, gate_ref, gfin_ref, op_ref, os_ref, buf, sem,
                    *, n_prompt_tiles):
    i = pl.program_id(0)
    n = pl.num_programs(0)
    tm = ROW_TM

    def row_copy(tile, slot, r, k):
        p = pos_ref[(tile * tm + r) * TOP_K + k]
        return pltpu.make_async_copy(y_hbm.at[pl.ds(p, 1)], buf.at[slot, k, pl.ds(r, 1)],
                                     sem.at[slot])

    def issue(tile, slot):
        def body(r, carry):
            for k in range(TOP_K):
                row_copy(tile, slot, r, k).start()
            return carry
        lax.fori_loop(0, tm, body, 0, unroll=4)

    def wait(tile, slot):
        def body(r, carry):
            for k in range(TOP_K):
                row_copy(tile, slot, r, k).wait()
            return carry
        lax.fori_loop(0, tm, body, 0, unroll=4)

    slot = i % 2

    @pl.when(i == 0)
    def _():
        issue(0, 0)

    @pl.when(i + 1 < n)
    def _():
        issue(i + 1, 1 - slot)

    wait(i, slot)
    gate = gate_ref[...]
    x2 = x1_ref[...]
    for k in range(TOP_K):
        x2 = x2 + gate[:, k:k + 1] * buf[slot, k]
    ms = jnp.mean(x2 * x2, axis=-1, keepdims=True)
    y = x2 * lax.rsqrt(ms + EPS) * gfin_ref[...]

    @pl.when(i < n_prompt_tiles)
    def _():
        op_ref[...] = y

    @pl.when(i >= n_prompt_tiles)
    def _():
        os_ref[...] = y


def _combine(pos, y_sorted, x1, gate, g_final, n_prompt, n_sample):
    tm = ROW_TM
    n_tok = x1.shape[0]
    n_prompt_tiles = n_prompt // tm
    n_sample_tiles = n_sample // tm

    grid_spec = pltpu.PrefetchScalarGridSpec(
        num_scalar_prefetch=1,
        grid=(n_tok // tm,),
        in_specs=[
            pl.BlockSpec(memory_space=pl.ANY),
            pl.BlockSpec((tm, D_MODEL), lambda i, pos: (i, 0)),
            pl.BlockSpec((tm, TOP_K), lambda i, pos: (i, 0)),
            pl.BlockSpec((1, D_MODEL), lambda i, pos: (0, 0)),
        ],
        out_specs=[
            pl.BlockSpec((tm, D_MODEL), lambda i, pos: (jnp.minimum(i, n_prompt_tiles - 1), 0)),
            pl.BlockSpec((tm, D_MODEL),
                         lambda i, pos: (jnp.clip(i - n_prompt_tiles, 0, n_sample_tiles - 1), 0)),
        ],
        scratch_shapes=[
            pltpu.VMEM((2, TOP_K, tm, D_MODEL), F32),
            pltpu.SemaphoreType.DMA((2,)),
        ],
    )
    return pl.pallas_call(
        functools.partial(_combine_kernel, n_prompt_tiles=n_prompt_tiles),
        grid_spec=grid_spec,
        out_shape=[
            jax.ShapeDtypeStruct((n_prompt, D_MODEL), F32),
            jax.ShapeDtypeStruct((n_sample, D_MODEL), F32),
        ],
        compiler_params=_cparams(("arbitrary",)),
        name="combine",
    )(pos, y_sorted, x1, gate, g_final.reshape(1, D_MODEL))


def kernel(x_prompt, x_sample, cache_k, cache_v, state_pool, meta_tokens, g_mix, w_in, sinks,
           w_pool_mix, pool_scale, w_br_attn, w_br_pool, w_out, g_ffn, w_router, b_router,
           w_gate_up, b_gate_up, w_down, b_down, g_final):
    depth = w_in.shape[0]
    assert depth == 1, "single-layer step only"
    batch, seq, _ = x_prompt.shape
    dec_batch, dec_seq, _ = x_sample.shape
    n_prompt = batch * seq
    n_sample = dec_batch * dec_seq
    n_tok = n_prompt + n_sample
    assert seq % ATTN_TILE == 0 and n_prompt % ROW_TM == 0 and n_sample % ROW_TM == 0
    assert dec_seq == SUBLANES and dec_batch % SAMPLE_SEQS == 0

    n_rows = -(-(n_tok + ATTN_TILE) // PROJ_TM) * PROJ_TM
    pad = jnp.zeros((n_rows - n_tok - N_META, D_MODEL), F32)
    x_all = jnp.concatenate([x_prompt.reshape(n_prompt, D_MODEL), x_sample.reshape(n_sample, D_MODEL),
                             pad, meta_tokens.astype(F32)], axis=0)

    p_all = _inproj(x_all, g_mix[0], w_in[0])

    ck = cache_k[0].reshape(dec_batch, WINDOW, KV_WIDTH)
    cv = cache_v[0].reshape(dec_batch, WINDOW, KV_WIDTH)
    state_pad = jnp.pad(state_pool[0], ((0, 0), (N_META - POOL_BUF, 0), (0, 0)))
    oa_p, z_p = _attn_prompt(p_all, sinks[0], batch, seq)
    oa_s, z_s = _attn_sample(p_all, sinks[0], ck, cv, state_pad, n_prompt, dec_batch, dec_seq)
    o_attn = jnp.concatenate([oa_p, oa_s], axis=0)
    z = jnp.concatenate([z_p, z_s], axis=0)

    wr = w_router[0]
    wr_hi = wr.astype(BF16)
    wr_lo = (wr - wr_hi.astype(F32)).astype(BF16)
    x1, h2, top_idx, gate, rank, counts = _finish(
        x_all, o_attn, z, p_all, w_pool_mix[0].astype(BF16), pool_scale[0],
        w_br_attn[0].astype(BF16), w_br_pool[0].astype(BF16), w_out[0].astype(BF16),
        g_ffn[0], jnp.concatenate([wr_hi, wr_lo], axis=1), b_router[0])

    n_assign = n_tok * TOP_K
    n_seg = N_EXPERTS + n_assign // MOE_MAXM
    cap = (n_assign // MOE_SUB + N_EXPERTS) * MOE_SUB
    seg_e, seg_ep, seg_row0, seg_nsub, total_sub, tok_sorted, dest = _route(
        top_idx, rank, counts, n_seg, cap)
    y_sorted = _moe(h2, seg_e, seg_ep, seg_row0, seg_nsub, total_sub, tok_sorted,
                    w_gate_up[0], b_gate_up[0], w_down[0], b_down[0], cap)
    y_p, y_s = _combine(dest, y_sorted, x1, gate, g_final, n_prompt, n_sample)

    k0, v0, u0, u1 = ATTN_WIDTH, ATTN_WIDTH + KV_WIDTH, ATTN_WIDTH + 2 * KV_WIDTH, IN_COLS - 2 * D_MODEL

    def tail_rows(n, c0, c1):
        return jnp.stack([p_all[(b + 1) * seq - n:(b + 1) * seq, c0:c1] for b in range(batch)])

    new_k_p = tail_rows(WINDOW, k0, v0).reshape(1, batch, WINDOW, N_KV_HEADS, HEAD_DIM)
    new_v_p = tail_rows(WINDOW, v0, u0).reshape(1, batch, WINDOW, N_KV_HEADS, HEAD_DIM)
    new_u_p = tail_rows(POOL_BUF, u0, u1)[None]
    ps = p_all[n_prompt:n_tok, k0:u1].reshape(dec_batch, dec_seq, u1 - k0)
    new_k_s = jnp.concatenate([ck[:, dec_seq:], ps[:, :, :KV_WIDTH]], axis=1).reshape(
        1, dec_batch, WINDOW, N_KV_HEADS, HEAD_DIM)
    new_v_s = jnp.concatenate([cv[:, dec_seq:], ps[:, :, KV_WIDTH:2 * KV_WIDTH]], axis=1).reshape(
        1, dec_batch, WINDOW, N_KV_HEADS, HEAD_DIM)
    new_u_s = jnp.concatenate([state_pool[0][:, dec_seq:], ps[:, :, 2 * KV_WIDTH:]], axis=1)[None]

    return (y_p.reshape(batch, seq, D_MODEL), y_s.reshape(dec_batch, dec_seq, D_MODEL),
            new_k_p, new_v_p, new_u_p, new_k_s, new_v_s, new_u_s)
```

```python
import functools

import jax
import jax.numpy as jnp
from jax import lax
from jax.experimental import pallas as pl
from jax.experimental.pallas import tpu as pltpu

F32 = jnp.float32
BF16 = jnp.bfloat16

D_MODEL = 2048
N_META = 16
N_HEADS = 32
N_KV_HEADS = 8
HEAD_DIM = 64
GROUP = N_HEADS // N_KV_HEADS
WINDOW = 128
ATTN_WIDTH = N_HEADS * HEAD_DIM
KV_WIDTH = N_KV_HEADS * HEAD_DIM
POOL_WIDTH = D_MODEL // 2
POOL_WINDOWS = (2, 4, 8, 16)
POOL_GROUP_DIM = POOL_WIDTH // len(POOL_WINDOWS)
POOL_BUF = max(POOL_WINDOWS) - 1
N_EXPERTS = 32
TOP_K = 4
D_FF = D_MODEL
SWIGLU_LIMIT = 7.0
SWIGLU_ALPHA = 1.702
EPS = 1e-5
IN_COLS = ATTN_WIDTH + 2 * KV_WIDTH + POOL_WIDTH + 2 * D_MODEL

LANES = 128
SUBLANES = 8
VMEM_LIMIT_BYTES = 56 * 1024 * 1024

ATTN_TILE = WINDOW
PROJ_TM = 512
PROJ_TN = 1024
ROW_TM = 256
SAMPLE_SEQS = 4
MOE_SUB = 256
MOE_MAXM = 6 * MOE_SUB
MOE_FA = 512
MOE_FB = 256
MOE_NFA = D_FF // MOE_FA
MOE_NFB = D_MODEL // MOE_FB
STAGE_PITCH = 3 * SUBLANES
NEG_BIG = -1e30

_SLOPES = tuple(float(2.0 ** (-8.0 * (i + 1) / N_HEADS)) for i in range(N_HEADS))


def _cparams(sem, vmem=VMEM_LIMIT_BYTES):
    return pltpu.CompilerParams(dimension_semantics=sem, vmem_limit_bytes=vmem)


def _rmsnorm(x, g):
    ms = jnp.mean(x * x, axis=-1, keepdims=True)
    return x * lax.rsqrt(ms + EPS) * g


def _inproj_kernel(xp_ref, xt_ref, g_ref, w_ref, o_ref, wbf_ref, *, n_main):
    m = pl.program_id(1)

    @pl.when(m == 0)
    def _():
        wbf_ref[...] = w_ref[...].astype(BF16)

    x = jnp.where(m < n_main, xp_ref[...], xt_ref[...])
    h = _rmsnorm(x, g_ref[...]).astype(BF16)
    o_ref[...] = jnp.dot(h, wbf_ref[...], preferred_element_type=F32)


def _inproj(x_main, x_tail, g_mix, w_in):
    n_main = x_main.shape[0] // PROJ_TM
    n_rows = x_main.shape[0] + x_tail.shape[0]
    grid = (IN_COLS // PROJ_TN, n_main + 1)
    return pl.pallas_call(
        functools.partial(_inproj_kernel, n_main=n_main),
        grid=grid,
        in_specs=[
            pl.BlockSpec((PROJ_TM, D_MODEL), lambda n, m: (jnp.minimum(m, n_main - 1), 0)),
            pl.BlockSpec((PROJ_TM, D_MODEL), lambda n, m: (0, 0)),
            pl.BlockSpec((1, D_MODEL), lambda n, m: (0, 0)),
            pl.BlockSpec((D_MODEL, PROJ_TN), lambda n, m: (0, n)),
        ],
        out_specs=pl.BlockSpec((PROJ_TM, PROJ_TN), lambda n, m: (m, n)),
        out_shape=jax.ShapeDtypeStruct((n_rows, IN_COLS), F32),
        scratch_shapes=[pltpu.VMEM((D_MODEL, PROJ_TN), BF16)],
        compiler_params=_cparams(("arbitrary", "arbitrary")),
        name="inproj",
    )(x_main, x_tail, g_mix.reshape(1, D_MODEL), w_in)


def _attn_bias(n_q, n_keys, first_key):
    r = jnp.arange(n_q, dtype=jnp.int32)[:, None]
    c = jnp.arange(n_keys, dtype=jnp.int32)[None, :]
    dist = r + WINDOW - c
    valid = (dist >= 0) & (dist <= WINDOW) & (c >= first_key)
    slopes = jnp.asarray(_SLOPES, F32).reshape(N_KV_HEADS, GROUP, 1, 1)
    bias = jnp.where(valid[None, None], -slopes * dist.astype(F32)[None, None], NEG_BIG)
    return bias.reshape(N_KV_HEADS, GROUP * n_q, n_keys)


def _group_sinks(sink_ref, kvh, n_q):
    g_row = lax.broadcasted_iota(jnp.int32, (GROUP * n_q, 1), 0) // n_q
    sink = jnp.zeros((GROUP * n_q, 1), F32)
    for g in range(GROUP):
        sink = jnp.where(g_row == g, sink_ref[kvh * GROUP + g], sink)
    return sink


def _group_attention(q, k, v, bias, sink_ref, kvh, n_q):
    kh = k[:, kvh * HEAD_DIM:(kvh + 1) * HEAD_DIM]
    vh = v[:, kvh * HEAD_DIM:(kvh + 1) * HEAD_DIM]
    heads = [kvh * GROUP + g for g in range(GROUP)]
    qg = jnp.concatenate([q[:, hd * HEAD_DIM:(hd + 1) * HEAD_DIM] for hd in heads], axis=0)
    s = lax.dot_general(qg, kh, (((1,), (1,)), ((), ())), preferred_element_type=F32) + bias
    sink = _group_sinks(sink_ref, kvh, n_q)
    m = jnp.maximum(jnp.max(s, axis=-1, keepdims=True), sink)
    p = jnp.exp(s - m)
    den = jnp.sum(p, axis=-1, keepdims=True) + jnp.exp(sink - m)
    o = jnp.dot(p.astype(BF16), vh, preferred_element_type=F32)
    return o / den


def _pool_features(ext, n_halo):
    outs = []
    for g, w in enumerate(POOL_WINDOWS):
        e = ext[:, g * POOL_GROUP_DIM:(g + 1) * POOL_GROUP_DIM]
        s = e
        shift = 1
        while shift < w:
            s = s + pltpu.roll(s, shift, axis=0)
            shift *= 2
        outs.append(s[n_halo:] * (1.0 / w) - e[n_halo:])
    return jnp.concatenate(outs, axis=1)


def _attn_prompt_kernel(sink_ref, bias_ref, q_ref, ko_ref, kp_ref, vo_ref, vp_ref, uo_ref, up_ref,
                        o_ref, z_ref, *, n_prompt_tiles):
    t = ATTN_TILE
    step = pl.program_id(0)

    @pl.when(step < n_prompt_tiles)
    def _():
        q = (q_ref[...] * (HEAD_DIM ** -0.5)).astype(BF16)
        k = jnp.concatenate([kp_ref[...], ko_ref[...]], axis=0).astype(BF16)
        v = jnp.concatenate([vp_ref[...], vo_ref[...]], axis=0).astype(BF16)
        for kvh in range(N_KV_HEADS):
            o = _group_attention(q, k, v, bias_ref[0, kvh], sink_ref, kvh, t)
            for g in range(GROUP):
                hd = kvh * GROUP + g
                o_ref[:, hd * HEAD_DIM:(hd + 1) * HEAD_DIM] = o[g * t:(g + 1) * t].astype(o_ref.dtype)

        ext = jnp.concatenate([up_ref[...], uo_ref[...]], axis=0)
        z_ref[...] = _pool_features(ext, N_META).astype(z_ref.dtype)

    @pl.when(step >= n_prompt_tiles)
    def _():
        o_ref[...] = jnp.zeros_like(o_ref)
        z_ref[...] = jnp.zeros_like(z_ref)


def _attn_prompt(p_all, sinks, batch, seq, n_out_rows):
    t = ATTN_TILE
    tiles = seq // t
    n_prompt_tiles = batch * tiles
    assert n_out_rows % t == 0
    meta_blk = p_all.shape[0] // t - 1
    q_w, kv_w, u_w = ATTN_WIDTH, KV_WIDTH, POOL_WIDTH
    kcol, vcol, ucol = ATTN_WIDTH // kv_w, ATTN_WIDTH // kv_w + 1, (ATTN_WIDTH + 2 * kv_w) // u_w
    halo_per_tile = t // N_META
    bias = jnp.stack([_attn_bias(t, 2 * t, t - N_META), _attn_bias(t, 2 * t, 0)])

    def prev(s):
        return jnp.where(s % tiles > 0, s - 1, meta_blk)

    return pl.pallas_call(
        functools.partial(_attn_prompt_kernel, n_prompt_tiles=n_prompt_tiles),
        grid=(n_out_rows // t,),
        in_specs=[
            pl.BlockSpec(memory_space=pltpu.SMEM),
            pl.BlockSpec((1, N_KV_HEADS, GROUP * t, 2 * t), lambda s: (jnp.minimum(s % tiles, 1), 0, 0, 0)),
            pl.BlockSpec((t, q_w), lambda s: (s, 0)),
            pl.BlockSpec((t, kv_w), lambda s: (s, kcol)),
            pl.BlockSpec((t, kv_w), lambda s: (prev(s), kcol)),
            pl.BlockSpec((t, kv_w), lambda s: (s, vcol)),
            pl.BlockSpec((t, kv_w), lambda s: (prev(s), vcol)),
            pl.BlockSpec((t, u_w), lambda s: (s, ucol)),
            pl.BlockSpec((N_META, u_w), lambda s: (prev(s) * halo_per_tile + halo_per_tile - 1, ucol)),
        ],
        out_specs=[
            pl.BlockSpec((t, q_w), lambda s: (s, 0)),
            pl.BlockSpec((t, u_w), lambda s: (s, 0)),
        ],
        out_shape=[
            jax.ShapeDtypeStruct((n_out_rows, q_w), BF16),
            jax.ShapeDtypeStruct((n_out_rows, u_w), BF16),
        ],
        compiler_params=_cparams(("arbitrary",)),
        name="attn_prompt",
    )(sinks, bias, p_all, p_all, p_all, p_all, p_all, p_all, p_all)


def _attn_sample_kernel(sink_ref, bias_ref, q_ref, kn_ref, vn_ref, un_ref, ck_ref, cv_ref, sp_ref,
                        o_prev_ref, z_prev_ref, o_ref, z_ref):
    del o_prev_ref, z_prev_ref
    n_seqs = ck_ref.shape[0]
    n_new = q_ref.shape[0] // n_seqs
    q_all = (q_ref[...] * (HEAD_DIM ** -0.5)).astype(BF16)
    for si in range(n_seqs):
        rows = slice(si * n_new, (si + 1) * n_new)
        q = q_all[rows]
        k = jnp.concatenate([ck_ref[si], kn_ref[rows, :]], axis=0).astype(BF16)
        v = jnp.concatenate([cv_ref[si], vn_ref[rows, :]], axis=0).astype(BF16)
        for kvh in range(N_KV_HEADS):
            o = _group_attention(q, k, v, bias_ref[kvh], sink_ref, kvh, n_new)
            for g in range(GROUP):
                hd = kvh * GROUP + g
                o_ref[rows, hd * HEAD_DIM:(hd + 1) * HEAD_DIM] = (
                    o[g * n_new:(g + 1) * n_new].astype(o_ref.dtype))
        ext = jnp.concatenate([sp_ref[si], un_ref[rows, :]], axis=0)
        z_ref[rows, :] = _pool_features(ext, sp_ref.shape[1]).astype(z_ref.dtype)


def _attn_sample(p_all, sinks, cache_k, cache_v, state_pad, o_attn, z, row0, dec_batch, dec_seq):
    q_w, kv_w, u_w = ATTN_WIDTH, KV_WIDTH, POOL_WIDTH
    kcol, vcol, ucol = ATTN_WIDTH // kv_w, ATTN_WIDTH // kv_w + 1, (ATTN_WIDTH + 2 * kv_w) // u_w
    ns = SAMPLE_SEQS
    rows = ns * dec_seq
    blk0 = row0 // rows
    n_halo = state_pad.shape[1]
    n_keys = WINDOW + dec_seq
    bias = _attn_bias(dec_seq, n_keys, 0)
    return pl.pallas_call(
        _attn_sample_kernel,
        grid=(dec_batch // ns,),
        in_specs=[
            pl.BlockSpec(memory_space=pltpu.SMEM),
            pl.BlockSpec((N_KV_HEADS, GROUP * dec_seq, n_keys), lambda b: (0, 0, 0)),
            pl.BlockSpec((rows, q_w), lambda b: (blk0 + b, 0)),
            pl.BlockSpec((rows, kv_w), lambda b: (blk0 + b, kcol)),
            pl.BlockSpec((rows, kv_w), lambda b: (blk0 + b, vcol)),
            pl.BlockSpec((rows, u_w), lambda b: (blk0 + b, ucol)),
            pl.BlockSpec((ns, WINDOW, kv_w), lambda b: (b, 0, 0)),
            pl.BlockSpec((ns, WINDOW, kv_w), lambda b: (b, 0, 0)),
            pl.BlockSpec((ns, n_halo, u_w), lambda b: (b, 0, 0)),
            pl.BlockSpec(memory_space=pl.ANY),
            pl.BlockSpec(memory_space=pl.ANY),
        ],
        out_specs=[
            pl.BlockSpec((rows, q_w), lambda b: (blk0 + b, 0)),
            pl.BlockSpec((rows, u_w), lambda b: (blk0 + b, 0)),
        ],
        out_shape=[
            jax.ShapeDtypeStruct(o_attn.shape, o_attn.dtype),
            jax.ShapeDtypeStruct(z.shape, z.dtype),
        ],
        input_output_aliases={9: 0, 10: 1},
        compiler_params=_cparams(("arbitrary",)),
        name="attn_sample",
    )(sinks, bias, p_all, p_all, p_all, p_all, cache_k, cache_v, state_pad, o_attn, z)


def _finish_kernel(xp_ref, xs_ref, oa_ref, z_ref, ag_ref, pg_ref, wpm_ref, ps_ref, wba_ref, wbp_ref,
                   wo_ref, gf_ref, wr_ref, br_ref,
                   x1_ref, h2_ref, idx_ref, gate_ref, rank_ref, cnt_ref, carry_ref,
                   *, n_prompt_tiles):
    tm = xp_ref.shape[0]
    i = pl.program_id(0)

    @pl.when(i == 0)
    def _():
        carry_ref[...] = jnp.zeros_like(carry_ref)

    x_in = jnp.where(i < n_prompt_tiles, xp_ref[...], xs_ref[...])

    z = z_ref[...]
    zp = jnp.concatenate(
        [jnp.dot(z[:, g * POOL_GROUP_DIM:(g + 1) * POOL_GROUP_DIM], wpm_ref[g],
                 preferred_element_type=F32) for g in range(len(POOL_WINDOWS))], axis=1)
    zp = (zp * ps_ref[...]).astype(BF16)
    ya = jnp.dot(oa_ref[...], wba_ref[...], preferred_element_type=F32)
    yp = jnp.dot(zp, wbp_ref[...], preferred_element_type=F32)
    merged = jax.nn.sigmoid(ag_ref[...]) * ya + jax.nn.sigmoid(pg_ref[...]) * yp
    x1 = x_in + jnp.dot(merged.astype(BF16), wo_ref[...], preferred_element_type=F32)
    x1_ref[...] = x1
    h2 = _rmsnorm(x1, gf_ref[...])
    n_lc = D_MODEL // LANES
    for c in range(n_lc):
        h2_ref[pl.ds(c, tm, stride=n_lc), :] = h2[:, c * LANES:(c + 1) * LANES]

    h_hi = h2.astype(BF16)
    h_lo = (h2 - h_hi.astype(F32)).astype(BF16)
    t = jnp.dot(h_hi, wr_ref[...], preferred_element_type=F32)
    logits = (t[:, :N_EXPERTS] + t[:, N_EXPERTS:]
              + jnp.dot(h_lo, wr_ref[:, :N_EXPERTS], preferred_element_type=F32) + br_ref[...])

    col = lax.broadcasted_iota(jnp.int32, logits.shape, 1).astype(F32)
    vals, idxs = [], []
    for _ in range(TOP_K):
        m = jnp.max(logits, axis=-1, keepdims=True)
        idx = jnp.min(jnp.where(logits == m, col, float(N_EXPERTS)), axis=-1, keepdims=True)
        vals.append(m)
        idxs.append(idx)
        logits = jnp.where(col == idx, -jnp.inf, logits)
    exps = [jnp.exp(v - vals[0]) for v in vals]
    den = exps[0] + exps[1] + exps[2] + exps[3]

    member = jnp.zeros(logits.shape, F32)
    for k in range(TOP_K):
        member = member + (col == idxs[k]).astype(F32)
    ri = lax.broadcasted_iota(jnp.int32, (tm, tm), 0)
    ci = lax.broadcasted_iota(jnp.int32, (tm, tm), 1)
    earlier = (ri > ci).astype(BF16)
    before = jnp.dot(earlier, member.astype(BF16), preferred_element_type=F32) + carry_ref[...]
    for k in range(TOP_K):
        idx_ref[:, k:k + 1] = idxs[k].astype(jnp.int32)
        gate_ref[:, k:k + 1] = exps[k] / den
        rank_ref[:, k:k + 1] = jnp.sum(jnp.where(col == idxs[k], before, 0.0), axis=-1,
                                       keepdims=True).astype(jnp.int32)
    carry_ref[...] += jnp.sum(member, axis=0, keepdims=True)
    cnt_ref[...] = carry_ref[...]


def _finish(x_prompt, x_sample, o_attn, z, p_all, wpm, pool_scale, wba, wbp, wo, g_ffn, w_router2,
            b_router):
    n_tok = o_attn.shape[0]
    tm = ROW_TM
    n_prompt_tiles = x_prompt.shape[0] // tm
    n_sample_tiles = x_sample.shape[0] // tm
    acol, pcol = (IN_COLS - 2 * D_MODEL) // D_MODEL, (IN_COLS - D_MODEL) // D_MODEL
    const = pl.Buffered(1)

    def whole(shape):
        nd = len(shape)
        return pl.BlockSpec(shape, lambda i: (0,) * nd, pipeline_mode=const)

    return pl.pallas_call(
        functools.partial(_finish_kernel, n_prompt_tiles=n_prompt_tiles),
        grid=(n_tok // tm,),
        in_specs=[
            pl.BlockSpec((tm, D_MODEL), lambda i: (jnp.minimum(i, n_prompt_tiles - 1), 0)),
            pl.BlockSpec((tm, D_MODEL),
                         lambda i: (jnp.clip(i - n_prompt_tiles, 0, n_sample_tiles - 1), 0)),
            pl.BlockSpec((tm, ATTN_WIDTH), lambda i: (i, 0)),
            pl.BlockSpec((tm, POOL_WIDTH), lambda i: (i, 0)),
            pl.BlockSpec((tm, D_MODEL), lambda i: (i, acol)),
            pl.BlockSpec((tm, D_MODEL), lambda i: (i, pcol)),
            whole(wpm.shape), whole((1, POOL_WIDTH)), whole(wba.shape), whole(wbp.shape),
            whole(wo.shape), whole((1, D_MODEL)), whole(w_router2.shape), whole((1, N_EXPERTS)),
        ],
        out_specs=[
            pl.BlockSpec((tm, D_MODEL), lambda i: (i, 0)),
            pl.BlockSpec((tm * (D_MODEL // LANES), LANES), lambda i: (i, 0)),
            pl.BlockSpec((tm, TOP_K), lambda i: (i, 0)),
            pl.BlockSpec((tm, TOP_K), lambda i: (i, 0)),
            pl.BlockSpec((tm, TOP_K), lambda i: (i, 0)),
            pl.BlockSpec((1, N_EXPERTS), lambda i: (0, 0)),
        ],
        out_shape=[
            jax.ShapeDtypeStruct((n_tok, D_MODEL), F32),
            jax.ShapeDtypeStruct((n_tok * (D_MODEL // LANES), LANES), F32),
            jax.ShapeDtypeStruct((n_tok, TOP_K), jnp.int32),
            jax.ShapeDtypeStruct((n_tok, TOP_K), F32),
            jax.ShapeDtypeStruct((n_tok, TOP_K), jnp.int32),
            jax.ShapeDtypeStruct((1, N_EXPERTS), F32),
        ],
        scratch_shapes=[pltpu.VMEM((1, N_EXPERTS), F32)],
        compiler_params=_cparams(("arbitrary",)),
        name="finish",
    )(x_prompt, x_sample, o_attn, z, p_all, p_all, wpm, pool_scale.reshape(1, POOL_WIDTH), wba, wbp,
      wo, g_ffn.reshape(1, D_MODEL), w_router2, b_router.reshape(1, N_EXPERTS))


def _route(top_idx, rank, counts_f, n_seg, cap):
    n_assign = top_idx.size
    counts = counts_f.reshape(N_EXPERTS).astype(jnp.int32)
    padded = (counts + MOE_SUB - 1) // MOE_SUB * MOE_SUB
    pend = jnp.cumsum(padded)
    pstart = pend - padded
    experts = jnp.arange(N_EXPERTS, dtype=jnp.int32)
    start_of = jnp.sum(jnp.where(top_idx[..., None] == experts, pstart, 0), axis=-1)
    dest = (start_of + rank).reshape(-1)
    tok_sorted = jnp.zeros((cap,), jnp.int32).at[dest].set(
        jnp.arange(n_assign, dtype=jnp.int32) // TOP_K)

    nseg_e = (counts + MOE_MAXM - 1) // MOE_MAXM
    seg_end = jnp.cumsum(nseg_e)
    seg_base = seg_end - nseg_e
    s_ids = jnp.arange(n_seg, dtype=jnp.int32)
    live = s_ids < seg_end[-1]
    last = jnp.maximum(seg_end[-1] - 1, 0)
    s_eff = jnp.where(live, s_ids, last)
    e_of_s = jnp.minimum(jnp.sum(s_eff[:, None] >= seg_end[None, :], axis=1), N_EXPERTS - 1)
    e_prev = jnp.concatenate([e_of_s[:1], e_of_s[:-1]])
    k_in = s_eff - seg_base[e_of_s]
    nrows = jnp.clip(counts[e_of_s] - k_in * MOE_MAXM, 0, MOE_MAXM)
    nsub = jnp.where(live, (nrows + MOE_SUB - 1) // MOE_SUB, 0)
    row0 = pstart[e_of_s] + k_in * MOE_MAXM
    total_sub = (pend[-1] // MOE_SUB).reshape(1)
    i32 = jnp.int32
    n_live = jnp.maximum(seg_end[-1], 1).astype(i32)
    return (n_live, e_of_s.astype(i32), e_prev.astype(i32), row0.astype(i32), nsub.astype(i32),
            total_sub.astype(i32), tok_sorted, dest.astype(i32))


def _for_pairs(n, fn):
    def pair(p, carry):
        fn(2 * p)
        fn(2 * p + 1)
        return carry
    lax.fori_loop(0, n // 2, pair, 0)

    @pl.when(n % 2 == 1)
    def _():
        fn(n - 1)


def _moe_kernel(seg_e, seg_ep, seg_row0, seg_nsub, total_sub, tok_ref,
                h2_hbm, wg_ref, wl_ref, wd_ref, bg_ref, bl_ref, bd_ref,
                y_hbm,
                xbuf, actbuf, stage, wab, wdb, ostage, pend, gsem, osem):
    s = pl.program_id(0)
    j = pl.program_id(1)
    nsub = seg_nsub[s]
    row0 = seg_row0[s]
    live = nsub > 0
    sub = MOE_SUB
    fa, fb = MOE_FA, MOE_FB
    n_lc = D_MODEL // LANES

    @pl.when(jnp.logical_and(s == 0, j == 0))
    def _init():
        pend[0] = 0
        pend[1] = 0

    def issue(chunk, slot):
        base = row0 + chunk * sub

        def body(r, carry):
            tok = tok_ref[base + r]
            pltpu.make_async_copy(
                h2_hbm.at[pl.ds(pl.multiple_of(tok * n_lc, n_lc), n_lc)],
                stage.at[slot, pl.ds(pl.multiple_of(r * STAGE_PITCH, SUBLANES), n_lc)],
                gsem.at[slot]).start(priority=1)
            return carry
        lax.fori_loop(0, sub, body, 0, unroll=8)

    def wait_chunk(slot):
        pltpu.make_async_copy(h2_hbm.at[pl.ds(0, sub * n_lc)], stage.at[slot, pl.ds(0, sub * n_lc)],
                              gsem.at[slot]).wait()

    def convert_chunk(c, slot):
        rows = pl.ds(pl.multiple_of(c * sub, sub), sub)
        for lc in range(n_lc):
            xbuf[rows, lc * LANES:(lc + 1) * LANES] = (
                stage[slot, pl.ds(lc, sub, stride=STAGE_PITCH), :].astype(BF16))

    @pl.when(jnp.logical_and(j == 0, live))
    def _gather():
        issue(0, 0)

        def chunk_body(c, carry):
            slot = c % 2

            @pl.when(c + 1 < nsub)
            def _():
                issue(c + 1, 1 - slot)
            wait_chunk(slot)
            convert_chunk(c, slot)
            return carry
        lax.fori_loop(0, nsub, chunk_body, 0)

    @pl.when(jnp.logical_and(j < MOE_NFA, live))
    def _up():
        wab[:, :fa] = wg_ref[0].astype(BF16)
        wab[:, fa:] = wl_ref[0].astype(BF16)
        bias = jnp.concatenate([bg_ref[0], bl_ref[0]], axis=1)

        def one(i):
            rows = pl.ds(pl.multiple_of(i * sub, sub), sub)
            gu = jnp.dot(xbuf[rows, :], wab[...], preferred_element_type=F32) + bias
            glu = jnp.minimum(gu[:, :fa], SWIGLU_LIMIT)
            lin = jnp.clip(gu[:, fa:], -SWIGLU_LIMIT, SWIGLU_LIMIT)
            actbuf[j, rows, :] = (glu * jax.nn.sigmoid(SWIGLU_ALPHA * glu) * (lin + 1.0)).astype(BF16)
        _for_pairs(nsub, one)

    def out_wait(slot):
        def body(i, carry):
            pltpu.make_async_copy(ostage.at[slot, pl.ds(0, sub)],
                                  y_hbm.at[pl.ds(0, sub), pl.ds(0, fb)], osem.at[slot]).wait()
            return carry
        lax.fori_loop(0, pend[slot], body, 0)
        pend[slot] = 0

    @pl.when(jnp.logical_and(j >= MOE_NFA, live))
    def _down():
        jb = j - MOE_NFA
        slot = jb % 2
        col0 = pl.multiple_of(jb * fb, fb)
        wdb[...] = wd_ref[0].astype(BF16)
        out_wait(slot)

        def one(i):
            r = pl.multiple_of(i * sub, sub)
            rows = pl.ds(r, sub)
            act = jnp.concatenate([actbuf[c, rows, :] for c in range(MOE_NFA)], axis=1)
            ostage[slot, rows, :] = jnp.dot(act, wdb[...], preferred_element_type=F32) + bd_ref[0]
            pltpu.make_async_copy(
                ostage.at[slot, rows],
                y_hbm.at[pl.ds(pl.multiple_of(row0 + r, sub), sub), pl.ds(col0, fb)],
                osem.at[slot]).start()
        _for_pairs(nsub, one)
        pend[slot] = nsub

    @pl.when(jnp.logical_and(s == pl.num_programs(0) - 1, j == pl.num_programs(1) - 1))
    def _final():
        out_wait(0)
        out_wait(1)
        n_tail = y_hbm.shape[0] // sub - total_sub[0]

        @pl.when(n_tail > 0)
        def _():
            ostage[0, pl.ds(0, sub), :] = jnp.zeros((sub, fb), F32)

            def fill(t, carry):
                r = pl.multiple_of((total_sub[0] + t) * sub, sub)
                for cb in range(MOE_NFB):
                    pltpu.make_async_copy(ostage.at[0, pl.ds(0, sub)],
                                          y_hbm.at[pl.ds(r, sub), pl.ds(cb * fb, fb)],
                                          osem.at[0]).start()
                return carry
            lax.fori_loop(0, n_tail, fill, 0)
            pend[0] = n_tail * MOE_NFB
            out_wait(0)


def _moe(h2, n_live, seg_e, seg_ep, seg_row0, seg_nsub, total_sub, tok_sorted, w_gate_up, b_gate_up,
         w_down, b_down, cap):
    nfa, nfb = MOE_NFA, MOE_NFB

    def up_chunk(s, j, nsub):
        return jnp.where(nsub[s] > 0, jnp.minimum(j, nfa - 1), nfa - 1)

    def wg_map(s, j, e, ep, r0, nsub, ts, tok):
        return (e[s], 0, up_chunk(s, j, nsub))

    def wl_map(s, j, e, ep, r0, nsub, ts, tok):
        return (e[s], 0, nfa + up_chunk(s, j, nsub))

    def wd_map(s, j, e, ep, r0, nsub, ts, tok):
        in_up = j < nfa
        eb = jnp.where(in_up, ep[s], e[s])
        jb = jnp.where(in_up, jnp.where(s > 0, nfb - 1, 0),
                       jnp.where(nsub[s] > 0, j - nfa, nfb - 1))
        return (eb, 0, jb)

    grid_spec = pltpu.PrefetchScalarGridSpec(
        num_scalar_prefetch=6,
        grid=(n_live, nfa + nfb),
        in_specs=[
            pl.BlockSpec(memory_space=pl.ANY),
            pl.BlockSpec((1, D_MODEL, MOE_FA), wg_map),
            pl.BlockSpec((1, D_MODEL, MOE_FA), wl_map),
            pl.BlockSpec((1, D_FF, MOE_FB), wd_map),
            pl.BlockSpec((1, 1, MOE_FA), wg_map),
            pl.BlockSpec((1, 1, MOE_FA), wl_map),
            pl.BlockSpec((1, 1, MOE_FB), wd_map),
        ],
        out_specs=pl.BlockSpec(memory_space=pl.ANY),
        scratch_shapes=[
            pltpu.VMEM((MOE_MAXM, D_MODEL), BF16),
            pltpu.VMEM((nfa, MOE_MAXM, MOE_FA), BF16),
            pltpu.VMEM((2, MOE_SUB * STAGE_PITCH, LANES), F32),
            pltpu.VMEM((D_MODEL, 2 * MOE_FA), BF16),
            pltpu.VMEM((D_FF, MOE_FB), BF16),
            pltpu.VMEM((2, MOE_MAXM, MOE_FB), F32),
            pltpu.SMEM((2,), jnp.int32),
            pltpu.SemaphoreType.DMA((2,)),
            pltpu.SemaphoreType.DMA((2,)),
        ],
    )
    return pl.pallas_call(
        _moe_kernel,
        grid_spec=grid_spec,
        out_shape=jax.ShapeDtypeStruct((cap, D_MODEL), F32),
        compiler_params=_cparams(("arbitrary", "arbitrary")),
        name="moe",
    )(seg_e, seg_ep, seg_row0, seg_nsub, total_sub, tok_sorted,
      h2, w_gate_up, w_gate_up, w_down,
      b_gate_up.reshape(N_EXPERTS, 1, 2 * D_FF), b_gate_up.reshape(N_EXPERTS, 1, 2 * D_FF),
      b_down.reshape(N_EXPERTS, 1, D_MODEL))


def _combine_kernel(pos_ref, y_hbm, x1_ref, gate_ref, gfin_ref, op_ref, os_ref, buf, sem,
                    *, n_prompt_tiles):
    i = pl.program_id(0)
    n = pl.num_programs(0)
    tm = ROW_TM

    def row_copy(tile, slot, r, k):
        p = pos_ref[(tile * tm + r) * TOP_K + k]
        return pltpu.make_async_copy(y_hbm.at[pl.ds(p, 1)], buf.at[slot, k, pl.ds(r, 1)],
                                     sem.at[slot])

    def issue(tile, slot):
        def body(r, carry):
            for k in range(TOP_K):
                row_copy(tile, slot, r, k).start()
            return carry
        lax.fori_loop(0, tm, body, 0, unroll=4)

    def wait(slot):
        for k in range(TOP_K):
            pltpu.make_async_copy(y_hbm.at[pl.ds(0, tm)], buf.at[slot, k], sem.at[slot]).wait()

    slot = i % 2

    @pl.when(i == 0)
    def _():
        issue(0, 0)

    @pl.when(i + 1 < n)
    def _():
        issue(i + 1, 1 - slot)

    wait(slot)
    gate = gate_ref[...]
    x2 = x1_ref[...]
    for k in range(TOP_K):
        x2 = x2 + gate[:, k:k + 1] * buf[slot, k]
    ms = jnp.mean(x2 * x2, axis=-1, keepdims=True)
    y = x2 * lax.rsqrt(ms + EPS) * gfin_ref[...]

    @pl.when(i < n_prompt_tiles)
    def _():
        op_ref[...] = y

    @pl.when(i >= n_prompt_tiles)
    def _():
        os_ref[...] = y


def _combine(pos, y_sorted, x1, gate, g_final, n_prompt, n_sample):
    tm = ROW_TM
    n_tok = x1.shape[0]
    n_prompt_tiles = n_prompt // tm
    n_sample_tiles = n_sample // tm

    grid_spec = pltpu.PrefetchScalarGridSpec(
        num_scalar_prefetch=1,
        grid=(n_tok // tm,),
        in_specs=[
            pl.BlockSpec(memory_space=pl.ANY),
            pl.BlockSpec((tm, D_MODEL), lambda i, pos: (i, 0)),
            pl.BlockSpec((tm, TOP_K), lambda i, pos: (i, 0)),
            pl.BlockSpec((1, D_MODEL), lambda i, pos: (0, 0)),
        ],
        out_specs=[
            pl.BlockSpec((tm, D_MODEL), lambda i, pos: (jnp.minimum(i, n_prompt_tiles - 1), 0)),
            pl.BlockSpec((tm, D_MODEL),
                         lambda i, pos: (jnp.clip(i - n_prompt_tiles, 0, n_sample_tiles - 1), 0)),
        ],
        scratch_shapes=[
            pltpu.VMEM((2, TOP_K, tm, D_MODEL), F32),
            pltpu.SemaphoreType.DMA((2,)),
        ],
    )
    return pl.pallas_call(
        functools.partial(_combine_kernel, n_prompt_tiles=n_prompt_tiles),
        grid_spec=grid_spec,
        out_shape=[
            jax.ShapeDtypeStruct((n_prompt, D_MODEL), F32),
            jax.ShapeDtypeStruct((n_sample, D_MODEL), F32),
        ],
        compiler_params=_cparams(("arbitrary",)),
        name="combine",
    )(pos, y_sorted, x1, gate, g_final.reshape(1, D_MODEL))


def kernel(x_prompt, x_sample, cache_k, cache_v, state_pool, meta_tokens, g_mix, w_in, sinks,
           w_pool_mix, pool_scale, w_br_attn, w_br_pool, w_out, g_ffn, w_router, b_router,
           w_gate_up, b_gate_up, w_down, b_down, g_final):
    depth = w_in.shape[0]
    assert depth == 1, "single-layer step only"
    batch, seq, _ = x_prompt.shape
    dec_batch, dec_seq, _ = x_sample.shape
    n_prompt = batch * seq
    n_sample = dec_batch * dec_seq
    n_tok = n_prompt + n_sample
    assert seq % ATTN_TILE == 0 and n_prompt % ROW_TM == 0 and n_sample % ROW_TM == 0
    assert dec_seq == SUBLANES and dec_batch % SAMPLE_SEQS == 0

    assert n_prompt % PROJ_TM == 0 and n_sample + ATTN_TILE <= PROJ_TM
    xp = x_prompt.reshape(n_prompt, D_MODEL)
    xs = x_sample.reshape(n_sample, D_MODEL)
    pad = jnp.zeros((PROJ_TM - n_sample - N_META, D_MODEL), F32)
    x_tail = jnp.concatenate([xs, pad, meta_tokens.astype(F32)], axis=0)

    p_all = _inproj(xp, x_tail, g_mix[0], w_in[0])

    ck = cache_k[0].reshape(dec_batch, WINDOW, KV_WIDTH)
    cv = cache_v[0].reshape(dec_batch, WINDOW, KV_WIDTH)
    state_pad = jnp.pad(state_pool[0], ((0, 0), (N_META - POOL_BUF, 0), (0, 0)))
    o_attn, z = _attn_prompt(p_all, sinks[0], batch, seq, n_tok)
    o_attn, z = _attn_sample(p_all, sinks[0], ck, cv, state_pad, o_attn, z, n_prompt, dec_batch,
                             dec_seq)

    wr = w_router[0]
    wr_hi = wr.astype(BF16)
    wr_lo = (wr - wr_hi.astype(F32)).astype(BF16)
    x1, h2, top_idx, gate, rank, counts = _finish(
        xp, xs, o_attn, z, p_all, w_pool_mix[0].astype(BF16), pool_scale[0],
        w_br_attn[0].astype(BF16), w_br_pool[0].astype(BF16), w_out[0].astype(BF16),
        g_ffn[0], jnp.concatenate([wr_hi, wr_lo], axis=1), b_router[0])

    n_assign = n_tok * TOP_K
    n_seg = N_EXPERTS + n_assign // MOE_MAXM
    cap = (n_assign // MOE_SUB + N_EXPERTS) * MOE_SUB
    n_live, seg_e, seg_ep, seg_row0, seg_nsub, total_sub, tok_sorted, dest = _route(
        top_idx, rank, counts, n_seg, cap)
    y_sorted = _moe(h2, n_live, seg_e, seg_ep, seg_row0, seg_nsub, total_sub, tok_sorted,
                    w_gate_up[0], b_gate_up[0], w_down[0], b_down[0], cap)
    y_p, y_s = _combine(dest, y_sorted, x1, gate, g_final, n_prompt, n_sample)

    k0, v0, u0, u1 = ATTN_WIDTH, ATTN_WIDTH + KV_WIDTH, ATTN_WIDTH + 2 * KV_WIDTH, IN_COLS - 2 * D_MODEL

    def tail_rows(n, c0, c1):
        return jnp.stack([p_all[(b + 1) * seq - n:(b + 1) * seq, c0:c1] for b in range(batch)])

    new_k_p = tail_rows(WINDOW, k0, v0).reshape(1, batch, WINDOW, N_KV_HEADS, HEAD_DIM)
    new_v_p = tail_rows(WINDOW, v0, u0).reshape(1, batch, WINDOW, N_KV_HEADS, HEAD_DIM)
    new_u_p = tail_rows(POOL_BUF, u0, u1)[None]
    ps = p_all[n_prompt:n_tok, k0:u1].reshape(dec_batch, dec_seq, u1 - k0)
    new_k_s = jnp.concatenate([ck[:, dec_seq:], ps[:, :, :KV_WIDTH]], axis=1).reshape(
        1, dec_batch, WINDOW, N_KV_HEADS, HEAD_DIM)
    new_v_s = jnp.concatenate([cv[:, dec_seq:], ps[:, :, KV_WIDTH:2 * KV_WIDTH]], axis=1).reshape(
        1, dec_batch, WINDOW, N_KV_HEADS, HEAD_DIM)
    new_u_s = jnp.concatenate([state_pool[0][:, dec_seq:], ps[:, :, 2 * KV_WIDTH:]], axis=1)[None]

    return (y_p.reshape(batch, seq, D_MODEL), y_s.reshape(dec_batch, dec_seq, D_MODEL),
            new_k_p, new_v_p, new_u_p, new_k_s, new_v_s, new_u_s)
```

```python
import functools

import jax
import jax.numpy as jnp
from jax import lax
from jax.experimental import pallas as pl
from jax.experimental.pallas import tpu as pltpu

F32 = jnp.float32
BF16 = jnp.bfloat16

D_MODEL = 2048
N_META = 16
N_HEADS = 32
N_KV_HEADS = 8
HEAD_DIM = 64
GROUP = N_HEADS // N_KV_HEADS
WINDOW = 128
ATTN_WIDTH = N_HEADS * HEAD_DIM
KV_WIDTH = N_KV_HEADS * HEAD_DIM
POOL_WIDTH = D_MODEL // 2
POOL_WINDOWS = (2, 4, 8, 16)
POOL_GROUP_DIM = POOL_WIDTH // len(POOL_WINDOWS)
POOL_BUF = max(POOL_WINDOWS) - 1
N_EXPERTS = 32
TOP_K = 4
D_FF = D_MODEL
SWIGLU_LIMIT = 7.0
SWIGLU_ALPHA = 1.702
EPS = 1e-5
IN_COLS = ATTN_WIDTH + 2 * KV_WIDTH + POOL_WIDTH + 2 * D_MODEL

LANES = 128
SUBLANES = 8
VMEM_LIMIT_BYTES = 56 * 1024 * 1024

ATTN_TILE = WINDOW
PROJ_TM = 512
PROJ_TN = 1024
ROW_TM = 256
SAMPLE_SEQS = 4
MOE_ROWS = 128
MOE_SUB = 2 * MOE_ROWS
MOE_MAXM = 12 * MOE_ROWS
MOE_FA = 512
MOE_FB = 512
MOE_NFA = D_FF // MOE_FA
MOE_NFB = D_MODEL // MOE_FB
STAGE_PITCH = 3 * SUBLANES
NEG_BIG = -1e30

_SLOPES = tuple(float(2.0 ** (-8.0 * (i + 1) / N_HEADS)) for i in range(N_HEADS))


def _cparams(sem, vmem=VMEM_LIMIT_BYTES):
    return pltpu.CompilerParams(dimension_semantics=sem, vmem_limit_bytes=vmem)


def _rmsnorm(x, g):
    ms = jnp.mean(x * x, axis=-1, keepdims=True)
    return x * lax.rsqrt(ms + EPS) * g


def _inproj_kernel(xp_ref, xt_ref, g_ref, w_ref, o_ref, wbf_ref, *, n_main):
    m = pl.program_id(1)

    @pl.when(m == 0)
    def _():
        wbf_ref[...] = w_ref[...].astype(BF16)

    x = jnp.where(m < n_main, xp_ref[...], xt_ref[...])
    h = _rmsnorm(x, g_ref[...]).astype(BF16)
    o_ref[...] = jnp.dot(h, wbf_ref[...], preferred_element_type=F32)


def _inproj(x_main, x_tail, g_mix, w_in):
    n_main = x_main.shape[0] // PROJ_TM
    n_rows = x_main.shape[0] + x_tail.shape[0]
    grid = (IN_COLS // PROJ_TN, n_main + 1)
    return pl.pallas_call(
        functools.partial(_inproj_kernel, n_main=n_main),
        grid=grid,
        in_specs=[
            pl.BlockSpec((PROJ_TM, D_MODEL), lambda n, m: (jnp.minimum(m, n_main - 1), 0)),
            pl.BlockSpec((PROJ_TM, D_MODEL), lambda n, m: (0, 0)),
            pl.BlockSpec((1, D_MODEL), lambda n, m: (0, 0)),
            pl.BlockSpec((D_MODEL, PROJ_TN), lambda n, m: (0, n)),
        ],
        out_specs=pl.BlockSpec((PROJ_TM, PROJ_TN), lambda n, m: (m, n)),
        out_shape=jax.ShapeDtypeStruct((n_rows, IN_COLS), F32),
        scratch_shapes=[pltpu.VMEM((D_MODEL, PROJ_TN), BF16)],
        compiler_params=_cparams(("arbitrary", "arbitrary")),
        name="inproj",
    )(x_main, x_tail, g_mix.reshape(1, D_MODEL), w_in)


def _attn_bias(n_q, n_keys, first_key):
    r = jnp.arange(n_q, dtype=jnp.int32)[:, None]
    c = jnp.arange(n_keys, dtype=jnp.int32)[None, :]
    dist = r + WINDOW - c
    valid = (dist >= 0) & (dist <= WINDOW) & (c >= first_key)
    slopes = jnp.asarray(_SLOPES, F32).reshape(N_KV_HEADS, GROUP, 1, 1)
    bias = jnp.where(valid[None, None], -slopes * dist.astype(F32)[None, None], NEG_BIG)
    return bias.reshape(N_KV_HEADS, GROUP * n_q, n_keys)


def _group_sinks(sink_ref, kvh, n_q):
    g_row = lax.broadcasted_iota(jnp.int32, (GROUP * n_q, 1), 0) // n_q
    sink = jnp.zeros((GROUP * n_q, 1), F32)
    for g in range(GROUP):
        sink = jnp.where(g_row == g, sink_ref[kvh * GROUP + g], sink)
    return sink


def _group_attention(q, k, v, bias, sink_ref, kvh, n_q):
    kh = k[:, kvh * HEAD_DIM:(kvh + 1) * HEAD_DIM]
    vh = v[:, kvh * HEAD_DIM:(kvh + 1) * HEAD_DIM]
    heads = [kvh * GROUP + g for g in range(GROUP)]
    qg = jnp.concatenate([q[:, hd * HEAD_DIM:(hd + 1) * HEAD_DIM] for hd in heads], axis=0)
    s = lax.dot_general(qg, kh, (((1,), (1,)), ((), ())), preferred_element_type=F32) + bias
    sink = _group_sinks(sink_ref, kvh, n_q)
    m = jnp.maximum(jnp.max(s, axis=-1, keepdims=True), sink)
    p = jnp.exp(s - m)
    den = jnp.sum(p, axis=-1, keepdims=True) + jnp.exp(sink - m)
    o = jnp.dot(p.astype(BF16), vh, preferred_element_type=F32)
    return o / den


def _pool_features(ext, n_halo):
    outs = []
    for g, w in enumerate(POOL_WINDOWS):
        e = ext[:, g * POOL_GROUP_DIM:(g + 1) * POOL_GROUP_DIM]
        s = e
        shift = 1
        while shift < w:
            s = s + pltpu.roll(s, shift, axis=0)
            shift *= 2
        outs.append(s[n_halo:] * (1.0 / w) - e[n_halo:])
    return jnp.concatenate(outs, axis=1)


def _attn_prompt_kernel(sink_ref, bias_ref, q_ref, ko_ref, kp_ref, vo_ref, vp_ref, uo_ref, up_ref,
                        o_ref, z_ref, *, n_prompt_tiles):
    t = ATTN_TILE
    step = pl.program_id(0)

    @pl.when(step < n_prompt_tiles)
    def _():
        q = (q_ref[...] * (HEAD_DIM ** -0.5)).astype(BF16)
        k = jnp.concatenate([kp_ref[...], ko_ref[...]], axis=0).astype(BF16)
        v = jnp.concatenate([vp_ref[...], vo_ref[...]], axis=0).astype(BF16)
        for kvh in range(N_KV_HEADS):
            o = _group_attention(q, k, v, bias_ref[0, kvh], sink_ref, kvh, t)
            for g in range(GROUP):
                hd = kvh * GROUP + g
                o_ref[:, hd * HEAD_DIM:(hd + 1) * HEAD_DIM] = o[g * t:(g + 1) * t].astype(o_ref.dtype)

        ext = jnp.concatenate([up_ref[...], uo_ref[...]], axis=0)
        z_ref[...] = _pool_features(ext, N_META).astype(z_ref.dtype)

    @pl.when(step >= n_prompt_tiles)
    def _():
        o_ref[...] = jnp.zeros_like(o_ref)
        z_ref[...] = jnp.zeros_like(z_ref)


def _attn_prompt(p_all, sinks, batch, seq, n_out_rows):
    t = ATTN_TILE
    tiles = seq // t
    n_prompt_tiles = batch * tiles
    assert n_out_rows % t == 0
    meta_blk = p_all.shape[0] // t - 1
    q_w, kv_w, u_w = ATTN_WIDTH, KV_WIDTH, POOL_WIDTH
    kcol, vcol, ucol = ATTN_WIDTH // kv_w, ATTN_WIDTH // kv_w + 1, (ATTN_WIDTH + 2 * kv_w) // u_w
    halo_per_tile = t // N_META
    bias = jnp.stack([_attn_bias(t, 2 * t, t - N_META), _attn_bias(t, 2 * t, 0)])

    def prev(s):
        return jnp.where(s % tiles > 0, s - 1, meta_blk)

    return pl.pallas_call(
        functools.partial(_attn_prompt_kernel, n_prompt_tiles=n_prompt_tiles),
        grid=(n_out_rows // t,),
        in_specs=[
            pl.BlockSpec(memory_space=pltpu.SMEM),
            pl.BlockSpec((1, N_KV_HEADS, GROUP * t, 2 * t), lambda s: (jnp.minimum(s % tiles, 1), 0, 0, 0)),
            pl.BlockSpec((t, q_w), lambda s: (s, 0)),
            pl.BlockSpec((t, kv_w), lambda s: (s, kcol)),
            pl.BlockSpec((t, kv_w), lambda s: (prev(s), kcol)),
            pl.BlockSpec((t, kv_w), lambda s: (s, vcol)),
            pl.BlockSpec((t, kv_w), lambda s: (prev(s), vcol)),
            pl.BlockSpec((t, u_w), lambda s: (s, ucol)),
            pl.BlockSpec((N_META, u_w), lambda s: (prev(s) * halo_per_tile + halo_per_tile - 1, ucol)),
        ],
        out_specs=[
            pl.BlockSpec((t, q_w), lambda s: (s, 0)),
            pl.BlockSpec((t, u_w), lambda s: (s, 0)),
        ],
        out_shape=[
            jax.ShapeDtypeStruct((n_out_rows, q_w), BF16),
            jax.ShapeDtypeStruct((n_out_rows, u_w), BF16),
        ],
        compiler_params=_cparams(("arbitrary",)),
        name="attn_prompt",
    )(sinks, bias, p_all, p_all, p_all, p_all, p_all, p_all, p_all)


def _attn_sample_kernel(sink_ref, bias_ref, q_ref, kn_ref, vn_ref, un_ref, ck_ref, cv_ref, sp_ref,
                        o_prev_ref, z_prev_ref, o_ref, z_ref):
    del o_prev_ref, z_prev_ref
    n_seqs = ck_ref.shape[0]
    n_new = q_ref.shape[0] // n_seqs
    q_all = (q_ref[...] * (HEAD_DIM ** -0.5)).astype(BF16)
    for si in range(n_seqs):
        rows = slice(si * n_new, (si + 1) * n_new)
        q = q_all[rows]
        k = jnp.concatenate([ck_ref[si], kn_ref[rows, :]], axis=0).astype(BF16)
        v = jnp.concatenate([cv_ref[si], vn_ref[rows, :]], axis=0).astype(BF16)
        for kvh in range(N_KV_HEADS):
            o = _group_attention(q, k, v, bias_ref[kvh], sink_ref, kvh, n_new)
            for g in range(GROUP):
                hd = kvh * GROUP + g
                o_ref[rows, hd * HEAD_DIM:(hd + 1) * HEAD_DIM] = (
                    o[g * n_new:(g + 1) * n_new].astype(o_ref.dtype))
        ext = jnp.concatenate([sp_ref[si], un_ref[rows, :]], axis=0)
        z_ref[rows, :] = _pool_features(ext, sp_ref.shape[1]).astype(z_ref.dtype)


def _attn_sample(p_all, sinks, cache_k, cache_v, state_pad, o_attn, z, row0, dec_batch, dec_seq):
    q_w, kv_w, u_w = ATTN_WIDTH, KV_WIDTH, POOL_WIDTH
    kcol, vcol, ucol = ATTN_WIDTH // kv_w, ATTN_WIDTH // kv_w + 1, (ATTN_WIDTH + 2 * kv_w) // u_w
    ns = SAMPLE_SEQS
    rows = ns * dec_seq
    blk0 = row0 // rows
    n_halo = state_pad.shape[1]
    n_keys = WINDOW + dec_seq
    bias = _attn_bias(dec_seq, n_keys, 0)
    return pl.pallas_call(
        _attn_sample_kernel,
        grid=(dec_batch // ns,),
        in_specs=[
            pl.BlockSpec(memory_space=pltpu.SMEM),
            pl.BlockSpec((N_KV_HEADS, GROUP * dec_seq, n_keys), lambda b: (0, 0, 0)),
            pl.BlockSpec((rows, q_w), lambda b: (blk0 + b, 0)),
            pl.BlockSpec((rows, kv_w), lambda b: (blk0 + b, kcol)),
            pl.BlockSpec((rows, kv_w), lambda b: (blk0 + b, vcol)),
            pl.BlockSpec((rows, u_w), lambda b: (blk0 + b, ucol)),
            pl.BlockSpec((ns, WINDOW, kv_w), lambda b: (b, 0, 0)),
            pl.BlockSpec((ns, WINDOW, kv_w), lambda b: (b, 0, 0)),
            pl.BlockSpec((ns, n_halo, u_w), lambda b: (b, 0, 0)),
            pl.BlockSpec(memory_space=pl.ANY),
            pl.BlockSpec(memory_space=pl.ANY),
        ],
        out_specs=[
            pl.BlockSpec((rows, q_w), lambda b: (blk0 + b, 0)),
            pl.BlockSpec((rows, u_w), lambda b: (blk0 + b, 0)),
        ],
        out_shape=[
            jax.ShapeDtypeStruct(o_attn.shape, o_attn.dtype),
            jax.ShapeDtypeStruct(z.shape, z.dtype),
        ],
        input_output_aliases={9: 0, 10: 1},
        compiler_params=_cparams(("arbitrary",)),
        name="attn_sample",
    )(sinks, bias, p_all, p_all, p_all, p_all, cache_k, cache_v, state_pad, o_attn, z)


def _finish_kernel(xp_ref, xs_ref, oa_ref, z_ref, ag_ref, pg_ref, wpm_ref, ps_ref, wba_ref, wbp_ref,
                   wo_ref, gf_ref, wr_ref, br_ref,
                   x1_ref, h2_ref, idx_ref, gate_ref, rank_ref, cnt_ref, carry_ref,
                   *, n_prompt_tiles):
    tm = xp_ref.shape[0]
    i = pl.program_id(0)

    @pl.when(i == 0)
    def _():
        carry_ref[...] = jnp.zeros_like(carry_ref)

    x_in = jnp.where(i < n_prompt_tiles, xp_ref[...], xs_ref[...])

    z = z_ref[...]
    zp = jnp.concatenate(
        [jnp.dot(z[:, g * POOL_GROUP_DIM:(g + 1) * POOL_GROUP_DIM], wpm_ref[g],
                 preferred_element_type=F32) for g in range(len(POOL_WINDOWS))], axis=1)
    zp = (zp * ps_ref[...]).astype(BF16)
    ya = jnp.dot(oa_ref[...], wba_ref[...], preferred_element_type=F32)
    yp = jnp.dot(zp, wbp_ref[...], preferred_element_type=F32)
    merged = jax.nn.sigmoid(ag_ref[...]) * ya + jax.nn.sigmoid(pg_ref[...]) * yp
    x1 = x_in + jnp.dot(merged.astype(BF16), wo_ref[...], preferred_element_type=F32)
    x1_ref[...] = x1
    h2 = _rmsnorm(x1, gf_ref[...])
    n_lc = D_MODEL // LANES
    for c in range(n_lc):
        h2_ref[pl.ds(c, tm, stride=n_lc), :] = h2[:, c * LANES:(c + 1) * LANES]

    h_hi = h2.astype(BF16)
    h_lo = (h2 - h_hi.astype(F32)).astype(BF16)
    t = jnp.dot(h_hi, wr_ref[...], preferred_element_type=F32)
    logits = (t[:, :N_EXPERTS] + t[:, N_EXPERTS:]
              + jnp.dot(h_lo, wr_ref[:, :N_EXPERTS], preferred_element_type=F32) + br_ref[...])

    col = lax.broadcasted_iota(jnp.int32, logits.shape, 1).astype(F32)
    vals, idxs = [], []
    for _ in range(TOP_K):
        m = jnp.max(logits, axis=-1, keepdims=True)
        idx = jnp.min(jnp.where(logits == m, col, float(N_EXPERTS)), axis=-1, keepdims=True)
        vals.append(m)
        idxs.append(idx)
        logits = jnp.where(col == idx, -jnp.inf, logits)
    exps = [jnp.exp(v - vals[0]) for v in vals]
    den = exps[0] + exps[1] + exps[2] + exps[3]

    member = jnp.zeros(logits.shape, F32)
    for k in range(TOP_K):
        member = member + (col == idxs[k]).astype(F32)
    ri = lax.broadcasted_iota(jnp.int32, (tm, tm), 0)
    ci = lax.broadcasted_iota(jnp.int32, (tm, tm), 1)
    earlier = (ri > ci).astype(BF16)
    before = jnp.dot(earlier, member.astype(BF16), preferred_element_type=F32) + carry_ref[...]
    for k in range(TOP_K):
        idx_ref[:, k:k + 1] = idxs[k].astype(jnp.int32)
        gate_ref[:, k:k + 1] = exps[k] / den
        rank_ref[:, k:k + 1] = jnp.sum(jnp.where(col == idxs[k], before, 0.0), axis=-1,
                                       keepdims=True).astype(jnp.int32)
    carry_ref[...] += jnp.sum(member, axis=0, keepdims=True)
    cnt_ref[...] = carry_ref[...]


def _finish(x_prompt, x_sample, o_attn, z, p_all, wpm, pool_scale, wba, wbp, wo, g_ffn, w_router2,
            b_router):
    n_tok = o_attn.shape[0]
    tm = ROW_TM
    n_prompt_tiles = x_prompt.shape[0] // tm
    n_sample_tiles = x_sample.shape[0] // tm
    acol, pcol = (IN_COLS - 2 * D_MODEL) // D_MODEL, (IN_COLS - D_MODEL) // D_MODEL
    const = pl.Buffered(1)

    def whole(shape):
        nd = len(shape)
        return pl.BlockSpec(shape, lambda i: (0,) * nd, pipeline_mode=const)

    return pl.pallas_call(
        functools.partial(_finish_kernel, n_prompt_tiles=n_prompt_tiles),
        grid=(n_tok // tm,),
        in_specs=[
            pl.BlockSpec((tm, D_MODEL), lambda i: (jnp.minimum(i, n_prompt_tiles - 1), 0)),
            pl.BlockSpec((tm, D_MODEL),
                         lambda i: (jnp.clip(i - n_prompt_tiles, 0, n_sample_tiles - 1), 0)),
            pl.BlockSpec((tm, ATTN_WIDTH), lambda i: (i, 0)),
            pl.BlockSpec((tm, POOL_WIDTH), lambda i: (i, 0)),
            pl.BlockSpec((tm, D_MODEL), lambda i: (i, acol)),
            pl.BlockSpec((tm, D_MODEL), lambda i: (i, pcol)),
            whole(wpm.shape), whole((1, POOL_WIDTH)), whole(wba.shape), whole(wbp.shape),
            whole(wo.shape), whole((1, D_MODEL)), whole(w_router2.shape), whole((1, N_EXPERTS)),
        ],
        out_specs=[
            pl.BlockSpec((tm, D_MODEL), lambda i: (i, 0)),
            pl.BlockSpec((tm * (D_MODEL // LANES), LANES), lambda i: (i, 0)),
            pl.BlockSpec((tm, TOP_K), lambda i: (i, 0)),
            pl.BlockSpec((tm, TOP_K), lambda i: (i, 0)),
            pl.BlockSpec((tm, TOP_K), lambda i: (i, 0)),
            pl.BlockSpec((1, N_EXPERTS), lambda i: (0, 0)),
        ],
        out_shape=[
            jax.ShapeDtypeStruct((n_tok, D_MODEL), F32),
            jax.ShapeDtypeStruct((n_tok * (D_MODEL // LANES), LANES), F32),
            jax.ShapeDtypeStruct((n_tok, TOP_K), jnp.int32),
            jax.ShapeDtypeStruct((n_tok, TOP_K), F32),
            jax.ShapeDtypeStruct((n_tok, TOP_K), jnp.int32),
            jax.ShapeDtypeStruct((1, N_EXPERTS), F32),
        ],
        scratch_shapes=[pltpu.VMEM((1, N_EXPERTS), F32)],
        compiler_params=_cparams(("arbitrary",)),
        name="finish",
    )(x_prompt, x_sample, o_attn, z, p_all, p_all, wpm, pool_scale.reshape(1, POOL_WIDTH), wba, wbp,
      wo, g_ffn.reshape(1, D_MODEL), w_router2, b_router.reshape(1, N_EXPERTS))


def _route(top_idx, rank, counts_f, n_seg, cap):
    n_assign = top_idx.size
    counts = counts_f.reshape(N_EXPERTS).astype(jnp.int32)
    padded = (counts + MOE_ROWS - 1) // MOE_ROWS * MOE_ROWS
    pend = jnp.cumsum(padded)
    pstart = pend - padded
    experts = jnp.arange(N_EXPERTS, dtype=jnp.int32)
    start_of = jnp.sum(jnp.where(top_idx[..., None] == experts, pstart, 0), axis=-1)
    dest = (start_of + rank).reshape(-1)
    tok_sorted = jnp.zeros((cap,), jnp.int32).at[dest].set(
        jnp.arange(n_assign, dtype=jnp.int32) // TOP_K)

    nseg_e = (counts + MOE_MAXM - 1) // MOE_MAXM
    seg_end = jnp.cumsum(nseg_e)
    seg_base = seg_end - nseg_e
    s_ids = jnp.arange(n_seg, dtype=jnp.int32)
    last = jnp.maximum(seg_end[-1] - 1, 0)
    s_eff = jnp.minimum(s_ids, last)
    e_of_s = jnp.minimum(jnp.sum(s_eff[:, None] >= seg_end[None, :], axis=1), N_EXPERTS - 1)
    e_prev = jnp.concatenate([e_of_s[:1], e_of_s[:-1]])
    is_last = s_ids >= last
    e_next = jnp.where(is_last, e_of_s, jnp.concatenate([e_of_s[1:], e_of_s[-1:]]))
    c_next = jnp.where(is_last, MOE_NFA - 1, 0)
    k_in = s_eff - seg_base[e_of_s]
    nrows = jnp.clip(counts[e_of_s] - k_in * MOE_MAXM, 0, MOE_MAXM)
    n_gran = (nrows + MOE_ROWS - 1) // MOE_ROWS
    row0 = pstart[e_of_s] + k_in * MOE_MAXM
    total_gran = (pend[-1] // MOE_ROWS).reshape(1)
    i32 = jnp.int32
    n_live = (last + 1).astype(i32)
    return (n_live, e_of_s.astype(i32), e_prev.astype(i32), e_next.astype(i32), c_next.astype(i32),
            row0.astype(i32), n_gran.astype(i32), total_gran.astype(i32), tok_sorted, dest.astype(i32))


def _for_blocks(n_gran, fn):
    n_full = n_gran // 2

    def pair(p, carry):
        r = p * (2 * MOE_SUB)
        fn(r, MOE_SUB)
        fn(r + MOE_SUB, MOE_SUB)
        return carry
    lax.fori_loop(0, n_full // 2, pair, 0)

    @pl.when(n_full % 2 == 1)
    def _():
        fn((n_full - 1) * MOE_SUB, MOE_SUB)

    @pl.when(n_gran % 2 == 1)
    def _():
        fn(n_full * MOE_SUB, MOE_ROWS)


def _moe_kernel(seg_e, seg_ep, seg_en, seg_cn, seg_row0, seg_ngran, total_gran, tok_ref,
                h2_hbm, wg_ref, wl_ref, wd_ref, bg_ref, bl_ref, bd_ref,
                y_hbm,
                xbuf, actbuf, stage, wab, wdb, ostage, pend, gsem, osem):
    del seg_e, seg_ep, seg_en, seg_cn
    s = pl.program_id(0)
    j = pl.program_id(1)
    n_gran = seg_ngran[s]
    row0 = seg_row0[s]
    live = n_gran > 0
    gran = MOE_ROWS
    fa, fb = MOE_FA, MOE_FB
    n_lc = D_MODEL // LANES

    def rows_at(first, n_rows):
        return pl.ds(pl.multiple_of(first, MOE_ROWS), n_rows)

    @pl.when(jnp.logical_and(s == 0, j == 0))
    def _init():
        pend[0] = 0
        pend[1] = 0

    def issue(chunk, slot):
        base = row0 + chunk * gran

        def body(r, carry):
            tok = tok_ref[base + r]
            pltpu.make_async_copy(
                h2_hbm.at[pl.ds(pl.multiple_of(tok * n_lc, n_lc), n_lc)],
                stage.at[slot, pl.ds(pl.multiple_of(r * STAGE_PITCH, SUBLANES), n_lc)],
                gsem.at[slot]).start(priority=1)
            return carry
        lax.fori_loop(0, gran, body, 0, unroll=8)

    def wait_chunk(slot):
        pltpu.make_async_copy(h2_hbm.at[pl.ds(0, gran * n_lc)], stage.at[slot, pl.ds(0, gran * n_lc)],
                              gsem.at[slot]).wait()

    def convert_chunk(c, slot):
        rows = rows_at(c * gran, gran)
        for lc in range(n_lc):
            xbuf[rows, lc * LANES:(lc + 1) * LANES] = (
                stage[slot, pl.ds(lc, gran, stride=STAGE_PITCH), :].astype(BF16))

    @pl.when(jnp.logical_and(j == 0, live))
    def _gather():
        issue(0, 0)

        def chunk_body(c, carry):
            slot = c % 2

            @pl.when(c + 1 < n_gran)
            def _():
                issue(c + 1, 1 - slot)
            wait_chunk(slot)
            convert_chunk(c, slot)
            return carry
        lax.fori_loop(0, n_gran, chunk_body, 0)

    @pl.when(jnp.logical_and(j < MOE_NFA, live))
    def _up():
        wab[:, :fa] = wg_ref[0].astype(BF16)
        wab[:, fa:] = wl_ref[0].astype(BF16)
        bias = jnp.concatenate([bg_ref[0], bl_ref[0]], axis=1)

        def one(first, n_rows):
            rows = rows_at(first, n_rows)
            gu = jnp.dot(xbuf[rows, :], wab[...], preferred_element_type=F32) + bias
            glu = jnp.minimum(gu[:, :fa], SWIGLU_LIMIT)
            lin = jnp.clip(gu[:, fa:], -SWIGLU_LIMIT, SWIGLU_LIMIT)
            actbuf[j, rows, :] = (glu * jax.nn.sigmoid(SWIGLU_ALPHA * glu) * (lin + 1.0)).astype(BF16)
        _for_blocks(n_gran, one)

    def out_copy(slot, src_row, dst_row, col0):
        return pltpu.make_async_copy(ostage.at[slot, rows_at(src_row, gran)],
                                     y_hbm.at[rows_at(dst_row, gran), pl.ds(col0, fb)], osem.at[slot])

    def out_wait(slot):
        def body(i, carry):
            out_copy(slot, 0, 0, 0).wait()
            return carry
        lax.fori_loop(0, pend[slot], body, 0)
        pend[slot] = 0

    @pl.when(jnp.logical_and(j >= MOE_NFA, live))
    def _down():
        jb = j - MOE_NFA
        slot = jb % 2
        col0 = pl.multiple_of(jb * fb, fb)
        wdb[...] = wd_ref[0].astype(BF16)
        out_wait(slot)

        def one(first, n_rows):
            rows = rows_at(first, n_rows)
            act = jnp.concatenate([actbuf[c, rows, :] for c in range(MOE_NFA)], axis=1)
            ostage[slot, rows, :] = jnp.dot(act, wdb[...], preferred_element_type=F32) + bd_ref[0]
            for h in range(n_rows // gran):
                out_copy(slot, first + h * gran, row0 + first + h * gran, col0).start()
        _for_blocks(n_gran, one)
        pend[slot] = n_gran

    @pl.when(jnp.logical_and(s == pl.num_programs(0) - 1, j == pl.num_programs(1) - 1))
    def _final():
        out_wait(0)
        out_wait(1)
        n_tail = y_hbm.shape[0] // gran - total_gran[0]

        @pl.when(n_tail > 0)
        def _():
            ostage[0, pl.ds(0, gran), :] = jnp.zeros((gran, fb), F32)

            def fill(t, carry):
                for cb in range(MOE_NFB):
                    out_copy(0, 0, (total_gran[0] + t) * gran, cb * fb).start()
                return carry
            lax.fori_loop(0, n_tail, fill, 0)
            pend[0] = n_tail * MOE_NFB
            out_wait(0)


def _moe(h2, n_live, seg_e, seg_ep, seg_en, seg_cn, seg_row0, seg_ngran, total_gran, tok_sorted,
         w_gate_up, b_gate_up, w_down, b_down, cap):
    nfa, nfb = MOE_NFA, MOE_NFB

    def up_block(s, j, e, en, cn):
        ahead = j >= nfa + nfb // 2
        eb = jnp.where(ahead, en[s], e[s])
        cb = jnp.where(j < nfa, j, jnp.where(ahead, cn[s], nfa - 1))
        return eb, cb

    def wg_map(s, j, e, ep, en, cn, r0, ng, tg, tok):
        eb, cb = up_block(s, j, e, en, cn)
        return (eb, 0, cb)

    def wl_map(s, j, e, ep, en, cn, r0, ng, tg, tok):
        eb, cb = up_block(s, j, e, en, cn)
        return (eb, 0, nfa + cb)

    def wd_map(s, j, e, ep, en, cn, r0, ng, tg, tok):
        in_up = j < nfa
        eb = jnp.where(in_up, ep[s], e[s])
        jb = jnp.where(in_up, jnp.where(s > 0, nfb - 1, 0), j - nfa)
        return (eb, 0, jb)

    grid_spec = pltpu.PrefetchScalarGridSpec(
        num_scalar_prefetch=8,
        grid=(n_live, nfa + nfb),
        in_specs=[
            pl.BlockSpec(memory_space=pl.ANY),
            pl.BlockSpec((1, D_MODEL, MOE_FA), wg_map),
            pl.BlockSpec((1, D_MODEL, MOE_FA), wl_map),
            pl.BlockSpec((1, D_FF, MOE_FB), wd_map),
            pl.BlockSpec((1, 1, MOE_FA), wg_map),
            pl.BlockSpec((1, 1, MOE_FA), wl_map),
            pl.BlockSpec((1, 1, MOE_FB), wd_map),
        ],
        out_specs=pl.BlockSpec(memory_space=pl.ANY),
        scratch_shapes=[
            pltpu.VMEM((MOE_MAXM, D_MODEL), BF16),
            pltpu.VMEM((nfa, MOE_MAXM, MOE_FA), BF16),
            pltpu.VMEM((2, MOE_ROWS * STAGE_PITCH, LANES), F32),
            pltpu.VMEM((D_MODEL, 2 * MOE_FA), BF16),
            pltpu.VMEM((D_FF, MOE_FB), BF16),
            pltpu.VMEM((2, MOE_MAXM, MOE_FB), F32),
            pltpu.SMEM((2,), jnp.int32),
            pltpu.SemaphoreType.DMA((2,)),
            pltpu.SemaphoreType.DMA((2,)),
        ],
    )
    return pl.pallas_call(
        _moe_kernel,
        grid_spec=grid_spec,
        out_shape=jax.ShapeDtypeStruct((cap, D_MODEL), F32),
        compiler_params=_cparams(("arbitrary", "arbitrary")),
        name="moe",
    )(seg_e, seg_ep, seg_en, seg_cn, seg_row0, seg_ngran, total_gran, tok_sorted,
      h2, w_gate_up, w_gate_up, w_down,
      b_gate_up.reshape(N_EXPERTS, 1, 2 * D_FF), b_gate_up.reshape(N_EXPERTS, 1, 2 * D_FF),
      b_down.reshape(N_EXPERTS, 1, D_MODEL))


def _combine_kernel(pos_ref, y_hbm, x1_ref, gate_ref, gfin_ref, op_ref, os_ref, buf, sem,
                    *, n_prompt_tiles):
    i = pl.program_id(0)
    n = pl.num_programs(0)
    tm = ROW_TM

    def row_copy(tile, slot, r, k):
        p = pos_ref[(tile * tm + r) * TOP_K + k]
        return pltpu.make_async_copy(y_hbm.at[pl.ds(p, 1)], buf.at[slot, k, pl.ds(r, 1)],
                                     sem.at[slot])

    def issue(tile, slot):
        def body(r, carry):
            for k in range(TOP_K):
                row_copy(tile, slot, r, k).start()
            return carry
        lax.fori_loop(0, tm, body, 0, unroll=4)

    def wait(slot):
        for k in range(TOP_K):
            pltpu.make_async_copy(y_hbm.at[pl.ds(0, tm)], buf.at[slot, k], sem.at[slot]).wait()

    slot = i % 2

    @pl.when(i == 0)
    def _():
        issue(0, 0)

    @pl.when(i + 1 < n)
    def _():
        issue(i + 1, 1 - slot)

    wait(slot)
    gate = gate_ref[...]
    x2 = x1_ref[...]
    for k in range(TOP_K):
        x2 = x2 + gate[:, k:k + 1] * buf[slot, k]
    ms = jnp.mean(x2 * x2, axis=-1, keepdims=True)
    y = x2 * lax.rsqrt(ms + EPS) * gfin_ref[...]

    @pl.when(i < n_prompt_tiles)
    def _():
        op_ref[...] = y

    @pl.when(i >= n_prompt_tiles)
    def _():
        os_ref[...] = y


def _combine(pos, y_sorted, x1, gate, g_final, n_prompt, n_sample):
    tm = ROW_TM
    n_tok = x1.shape[0]
    n_prompt_tiles = n_prompt // tm
    n_sample_tiles = n_sample // tm

    grid_spec = pltpu.PrefetchScalarGridSpec(
        num_scalar_prefetch=1,
        grid=(n_tok // tm,),
        in_specs=[
            pl.BlockSpec(memory_space=pl.ANY),
            pl.BlockSpec((tm, D_MODEL), lambda i, pos: (i, 0)),
            pl.BlockSpec((tm, TOP_K), lambda i, pos: (i, 0)),
            pl.BlockSpec((1, D_MODEL), lambda i, pos: (0, 0)),
        ],
        out_specs=[
            pl.BlockSpec((tm, D_MODEL), lambda i, pos: (jnp.minimum(i, n_prompt_tiles - 1), 0)),
            pl.BlockSpec((tm, D_MODEL),
                         lambda i, pos: (jnp.clip(i - n_prompt_tiles, 0, n_sample_tiles - 1), 0)),
        ],
        scratch_shapes=[
            pltpu.VMEM((2, TOP_K, tm, D_MODEL), F32),
            pltpu.SemaphoreType.DMA((2,)),
        ],
    )
    return pl.pallas_call(
        functools.partial(_combine_kernel, n_prompt_tiles=n_prompt_tiles),
        grid_spec=grid_spec,
        out_shape=[
            jax.ShapeDtypeStruct((n_prompt, D_MODEL), F32),
            jax.ShapeDtypeStruct((n_sample, D_MODEL), F32),
        ],
        compiler_params=_cparams(("arbitrary",)),
        name="combine",
    )(pos, y_sorted, x1, gate, g_final.reshape(1, D_MODEL))


def kernel(x_prompt, x_sample, cache_k, cache_v, state_pool, meta_tokens, g_mix, w_in, sinks,
           w_pool_mix, pool_scale, w_br_attn, w_br_pool, w_out, g_ffn, w_router, b_router,
           w_gate_up, b_gate_up, w_down, b_down, g_final):
    depth = w_in.shape[0]
    assert depth == 1, "single-layer step only"
    batch, seq, _ = x_prompt.shape
    dec_batch, dec_seq, _ = x_sample.shape
    n_prompt = batch * seq
    n_sample = dec_batch * dec_seq
    n_tok = n_prompt + n_sample
    assert seq % ATTN_TILE == 0 and n_prompt % ROW_TM == 0 and n_sample % ROW_TM == 0
    assert dec_seq == SUBLANES and dec_batch % SAMPLE_SEQS == 0

    assert n_prompt % PROJ_TM == 0 and n_sample + ATTN_TILE <= PROJ_TM
    xp = x_prompt.reshape(n_prompt, D_MODEL)
    xs = x_sample.reshape(n_sample, D_MODEL)
    pad = jnp.zeros((PROJ_TM - n_sample - N_META, D_MODEL), F32)
    x_tail = jnp.concatenate([xs, pad, meta_tokens.astype(F32)], axis=0)

    p_all = _inproj(xp, x_tail, g_mix[0], w_in[0])

    ck = cache_k[0].reshape(dec_batch, WINDOW, KV_WIDTH)
    cv = cache_v[0].reshape(dec_batch, WINDOW, KV_WIDTH)
    state_pad = jnp.pad(state_pool[0], ((0, 0), (N_META - POOL_BUF, 0), (0, 0)))
    o_attn, z = _attn_prompt(p_all, sinks[0], batch, seq, n_tok)
    o_attn, z = _attn_sample(p_all, sinks[0], ck, cv, state_pad, o_attn, z, n_prompt, dec_batch,
                             dec_seq)

    wr = w_router[0]
    wr_hi = wr.astype(BF16)
    wr_lo = (wr - wr_hi.astype(F32)).astype(BF16)
    x1, h2, top_idx, gate, rank, counts = _finish(
        xp, xs, o_attn, z, p_all, w_pool_mix[0].astype(BF16), pool_scale[0],
        w_br_attn[0].astype(BF16), w_br_pool[0].astype(BF16), w_out[0].astype(BF16),
        g_ffn[0], jnp.concatenate([wr_hi, wr_lo], axis=1), b_router[0])

    n_assign = n_tok * TOP_K
    n_seg = N_EXPERTS + n_assign // MOE_MAXM
    cap = (n_assign // MOE_ROWS + N_EXPERTS) * MOE_ROWS
    n_live, seg_e, seg_ep, seg_en, seg_cn, seg_row0, seg_ngran, total_gran, tok_sorted, dest = _route(
        top_idx, rank, counts, n_seg, cap)
    y_sorted = _moe(h2, n_live, seg_e, seg_ep, seg_en, seg_cn, seg_row0, seg_ngran, total_gran,
                    tok_sorted, w_gate_up[0], b_gate_up[0], w_down[0], b_down[0], cap)
    y_p, y_s = _combine(dest, y_sorted, x1, gate, g_final, n_prompt, n_sample)

    k0, v0, u0, u1 = ATTN_WIDTH, ATTN_WIDTH + KV_WIDTH, ATTN_WIDTH + 2 * KV_WIDTH, IN_COLS - 2 * D_MODEL

    def tail_rows(n, c0, c1):
        return jnp.stack([p_all[(b + 1) * seq - n:(b + 1) * seq, c0:c1] for b in range(batch)])

    new_k_p = tail_rows(WINDOW, k0, v0).reshape(1, batch, WINDOW, N_KV_HEADS, HEAD_DIM)
    new_v_p = tail_rows(WINDOW, v0, u0).reshape(1, batch, WINDOW, N_KV_HEADS, HEAD_DIM)
    new_u_p = tail_rows(POOL_BUF, u0, u1)[None]
    ps = p_all[n_prompt:n_tok, k0:u1].reshape(dec_batch, dec_seq, u1 - k0)
    new_k_s = jnp.concatenate([ck[:, dec_seq:], ps[:, :, :KV_WIDTH]], axis=1).reshape(
        1, dec_batch, WINDOW, N_KV_HEADS, HEAD_DIM)
    new_v_s = jnp.concatenate([cv[:, dec_seq:], ps[:, :, KV_WIDTH:2 * KV_WIDTH]], axis=1).reshape(
        1, dec_batch, WINDOW, N_KV_HEADS, HEAD_DIM)
    new_u_s = jnp.concatenate([state_pool[0][:, dec_seq:], ps[:, :, 2 * KV_WIDTH:]], axis=1)[None]

    return (y_p.reshape(batch, seq, D_MODEL), y_s.reshape(dec_batch, dec_seq, D_MODEL),
            new_k_p, new_v_p, new_u_p, new_k_s, new_v_s, new_u_s)
```

```python
import functools

import jax
import jax.numpy as jnp
from jax import lax
from jax.experimental import pallas as pl
from jax.experimental.pallas import tpu as pltpu

F32 = jnp.float32
BF16 = jnp.bfloat16

D_MODEL = 2048
N_META = 16
N_HEADS = 32
N_KV_HEADS = 8
HEAD_DIM = 64
GROUP = N_HEADS // N_KV_HEADS
WINDOW = 128
ATTN_WIDTH = N_HEADS * HEAD_DIM
KV_WIDTH = N_KV_HEADS * HEAD_DIM
POOL_WIDTH = D_MODEL // 2
POOL_WINDOWS = (2, 4, 8, 16)
POOL_GROUP_DIM = POOL_WIDTH // len(POOL_WINDOWS)
POOL_BUF = max(POOL_WINDOWS) - 1
N_EXPERTS = 32
TOP_K = 4
D_FF = D_MODEL
SWIGLU_LIMIT = 7.0
SWIGLU_ALPHA = 1.702
EPS = 1e-5
IN_COLS = ATTN_WIDTH + 2 * KV_WIDTH + POOL_WIDTH + 2 * D_MODEL

LANES = 128
SUBLANES = 8
VMEM_LIMIT_BYTES = 56 * 1024 * 1024

ATTN_TILE = WINDOW
PROJ_TM = 512
PROJ_TN = 1024
ROW_TM = 256
SAMPLE_SEQS = 4
MOE_ROWS = 128
MOE_SUB = 2 * MOE_ROWS
MOE_MAXM = 12 * MOE_ROWS
MOE_FA = 512
MOE_FB = 512
MOE_GATHER = 2 * MOE_ROWS
MOE_NFA = D_FF // MOE_FA
MOE_NFB = D_MODEL // MOE_FB
STAGE_PITCH = 3 * SUBLANES
NEG_BIG = -1e30

_SLOPES = tuple(float(2.0 ** (-8.0 * (i + 1) / N_HEADS)) for i in range(N_HEADS))


def _cparams(sem, vmem=VMEM_LIMIT_BYTES):
    return pltpu.CompilerParams(dimension_semantics=sem, vmem_limit_bytes=vmem)


def _rmsnorm(x, g):
    ms = jnp.mean(x * x, axis=-1, keepdims=True)
    return x * lax.rsqrt(ms + EPS) * g


def _prenorm_kernel(xp_ref, xt_ref, g_ref, h_ref, *, n_main):
    x = jnp.where(pl.program_id(0) < n_main, xp_ref[...], xt_ref[...])
    h_ref[...] = _rmsnorm(x, g_ref[...]).astype(h_ref.dtype)


def _prenorm(x_main, x_tail, g_mix):
    n_main = x_main.shape[0] // PROJ_TM
    n_rows = x_main.shape[0] + x_tail.shape[0]
    return pl.pallas_call(
        functools.partial(_prenorm_kernel, n_main=n_main),
        grid=(n_main + 1,),
        in_specs=[
            pl.BlockSpec((PROJ_TM, D_MODEL), lambda m: (jnp.minimum(m, n_main - 1), 0)),
            pl.BlockSpec((PROJ_TM, D_MODEL), lambda m: (0, 0)),
            pl.BlockSpec((1, D_MODEL), lambda m: (0, 0)),
        ],
        out_specs=pl.BlockSpec((PROJ_TM, D_MODEL), lambda m: (m, 0)),
        out_shape=jax.ShapeDtypeStruct((n_rows, D_MODEL), BF16),
        compiler_params=_cparams(("arbitrary",)),
        name="prenorm",
    )(x_main, x_tail, g_mix.reshape(1, D_MODEL))


def _inproj_kernel(h_ref, w_ref, o_ref, wbf_ref):
    @pl.when(pl.program_id(1) == 0)
    def _():
        wbf_ref[...] = w_ref[...].astype(BF16)

    o_ref[...] = jnp.dot(h_ref[...], wbf_ref[...], preferred_element_type=F32)


def _inproj(h, w_in):
    n_rows = h.shape[0]
    grid = (IN_COLS // PROJ_TN, n_rows // PROJ_TM)
    return pl.pallas_call(
        _inproj_kernel,
        grid=grid,
        in_specs=[
            pl.BlockSpec((PROJ_TM, D_MODEL), lambda n, m: (m, 0)),
            pl.BlockSpec((D_MODEL, PROJ_TN), lambda n, m: (0, n)),
        ],
        out_specs=pl.BlockSpec((PROJ_TM, PROJ_TN), lambda n, m: (m, n)),
        out_shape=jax.ShapeDtypeStruct((n_rows, IN_COLS), F32),
        scratch_shapes=[pltpu.VMEM((D_MODEL, PROJ_TN), BF16)],
        compiler_params=_cparams(("arbitrary", "arbitrary")),
        name="inproj",
    )(h, w_in)


def _attn_bias(n_q, n_keys, first_key):
    r = jnp.arange(n_q, dtype=jnp.int32)[:, None]
    c = jnp.arange(n_keys, dtype=jnp.int32)[None, :]
    dist = r + WINDOW - c
    valid = (dist >= 0) & (dist <= WINDOW) & (c >= first_key)
    slopes = jnp.asarray(_SLOPES, F32).reshape(N_KV_HEADS, GROUP, 1, 1)
    bias = jnp.where(valid[None, None], -slopes * dist.astype(F32)[None, None], NEG_BIG)
    return bias.reshape(N_KV_HEADS, GROUP * n_q, n_keys)


def _group_sinks(sink_ref, kvh, n_q):
    g_row = lax.broadcasted_iota(jnp.int32, (GROUP * n_q, 1), 0) // n_q
    sink = jnp.zeros((GROUP * n_q, 1), F32)
    for g in range(GROUP):
        sink = jnp.where(g_row == g, sink_ref[kvh * GROUP + g], sink)
    return sink


def _group_attention(q, k, v, bias, sink_ref, kvh, n_q):
    kh = k[:, kvh * HEAD_DIM:(kvh + 1) * HEAD_DIM]
    vh = v[:, kvh * HEAD_DIM:(kvh + 1) * HEAD_DIM]
    heads = [kvh * GROUP + g for g in range(GROUP)]
    qg = jnp.concatenate([q[:, hd * HEAD_DIM:(hd + 1) * HEAD_DIM] for hd in heads], axis=0)
    s = lax.dot_general(qg, kh, (((1,), (1,)), ((), ())), preferred_element_type=F32) + bias
    sink = _group_sinks(sink_ref, kvh, n_q)
    m = jnp.maximum(jnp.max(s, axis=-1, keepdims=True), sink)
    p = jnp.exp(s - m)
    den = jnp.sum(p, axis=-1, keepdims=True) + jnp.exp(sink - m)
    o = jnp.dot(p.astype(BF16), vh, preferred_element_type=F32)
    return o / den


def _pool_features(ext, n_halo):
    outs = []
    for g, w in enumerate(POOL_WINDOWS):
        e = ext[:, g * POOL_GROUP_DIM:(g + 1) * POOL_GROUP_DIM]
        s = e
        shift = 1
        while shift < w:
            s = s + pltpu.roll(s, shift, axis=0)
            shift *= 2
        outs.append(s[n_halo:] * (1.0 / w) - e[n_halo:])
    return jnp.concatenate(outs, axis=1)


def _attn_prompt_kernel(sink_ref, bias_ref, q_ref, ko_ref, kp_ref, vo_ref, vp_ref, uo_ref, up_ref,
                        o_ref, z_ref, *, n_prompt_tiles):
    t = ATTN_TILE
    step = pl.program_id(0)

    @pl.when(step < n_prompt_tiles)
    def _():
        q = (q_ref[...] * (HEAD_DIM ** -0.5)).astype(BF16)
        k = jnp.concatenate([kp_ref[...], ko_ref[...]], axis=0).astype(BF16)
        v = jnp.concatenate([vp_ref[...], vo_ref[...]], axis=0).astype(BF16)
        for kvh in range(N_KV_HEADS):
            o = _group_attention(q, k, v, bias_ref[0, kvh], sink_ref, kvh, t)
            for g in range(GROUP):
                hd = kvh * GROUP + g
                o_ref[:, hd * HEAD_DIM:(hd + 1) * HEAD_DIM] = o[g * t:(g + 1) * t].astype(o_ref.dtype)

        ext = jnp.concatenate([up_ref[...], uo_ref[...]], axis=0)
        z_ref[...] = _pool_features(ext, N_META).astype(z_ref.dtype)

    @pl.when(step >= n_prompt_tiles)
    def _():
        o_ref[...] = jnp.zeros_like(o_ref)
        z_ref[...] = jnp.zeros_like(z_ref)


def _attn_prompt(p_all, sinks, batch, seq, n_out_rows):
    t = ATTN_TILE
    tiles = seq // t
    n_prompt_tiles = batch * tiles
    assert n_out_rows % t == 0
    meta_blk = p_all.shape[0] // t - 1
    q_w, kv_w, u_w = ATTN_WIDTH, KV_WIDTH, POOL_WIDTH
    kcol, vcol, ucol = ATTN_WIDTH // kv_w, ATTN_WIDTH // kv_w + 1, (ATTN_WIDTH + 2 * kv_w) // u_w
    halo_per_tile = t // N_META
    bias = jnp.stack([_attn_bias(t, 2 * t, t - N_META), _attn_bias(t, 2 * t, 0)])

    def prev(s):
        return jnp.where(s % tiles > 0, s - 1, meta_blk)

    return pl.pallas_call(
        functools.partial(_attn_prompt_kernel, n_prompt_tiles=n_prompt_tiles),
        grid=(n_out_rows // t,),
        in_specs=[
            pl.BlockSpec(memory_space=pltpu.SMEM),
            pl.BlockSpec((1, N_KV_HEADS, GROUP * t, 2 * t), lambda s: (jnp.minimum(s % tiles, 1), 0, 0, 0)),
            pl.BlockSpec((t, q_w), lambda s: (s, 0)),
            pl.BlockSpec((t, kv_w), lambda s: (s, kcol)),
            pl.BlockSpec((t, kv_w), lambda s: (prev(s), kcol)),
            pl.BlockSpec((t, kv_w), lambda s: (s, vcol)),
            pl.BlockSpec((t, kv_w), lambda s: (prev(s), vcol)),
            pl.BlockSpec((t, u_w), lambda s: (s, ucol)),
            pl.BlockSpec((N_META, u_w), lambda s: (prev(s) * halo_per_tile + halo_per_tile - 1, ucol)),
        ],
        out_specs=[
            pl.BlockSpec((t, q_w), lambda s: (s, 0)),
            pl.BlockSpec((t, u_w), lambda s: (s, 0)),
        ],
        out_shape=[
            jax.ShapeDtypeStruct((n_out_rows, q_w), BF16),
            jax.ShapeDtypeStruct((n_out_rows, u_w), BF16),
        ],
        compiler_params=_cparams(("arbitrary",)),
        name="attn_prompt",
    )(sinks, bias, p_all, p_all, p_all, p_all, p_all, p_all, p_all)


def _attn_sample_kernel(sink_ref, bias_ref, q_ref, kn_ref, vn_ref, un_ref, ck_ref, cv_ref, sp_ref,
                        o_prev_ref, z_prev_ref, o_ref, z_ref):
    del o_prev_ref, z_prev_ref
    n_seqs = ck_ref.shape[0]
    n_new = q_ref.shape[0] // n_seqs
    q_all = (q_ref[...] * (HEAD_DIM ** -0.5)).astype(BF16)
    for si in range(n_seqs):
        rows = slice(si * n_new, (si + 1) * n_new)
        q = q_all[rows]
        k = jnp.concatenate([ck_ref[si], kn_ref[rows, :]], axis=0).astype(BF16)
        v = jnp.concatenate([cv_ref[si], vn_ref[rows, :]], axis=0).astype(BF16)
        for kvh in range(N_KV_HEADS):
            o = _group_attention(q, k, v, bias_ref[kvh], sink_ref, kvh, n_new)
            for g in range(GROUP):
                hd = kvh * GROUP + g
                o_ref[rows, hd * HEAD_DIM:(hd + 1) * HEAD_DIM] = (
                    o[g * n_new:(g + 1) * n_new].astype(o_ref.dtype))
        ext = jnp.concatenate([sp_ref[si], un_ref[rows, :]], axis=0)
        z_ref[rows, :] = _pool_features(ext, sp_ref.shape[1]).astype(z_ref.dtype)


def _attn_sample(p_all, sinks, cache_k, cache_v, state_pad, o_attn, z, row0, dec_batch, dec_seq):
    q_w, kv_w, u_w = ATTN_WIDTH, KV_WIDTH, POOL_WIDTH
    kcol, vcol, ucol = ATTN_WIDTH // kv_w, ATTN_WIDTH // kv_w + 1, (ATTN_WIDTH + 2 * kv_w) // u_w
    ns = SAMPLE_SEQS
    rows = ns * dec_seq
    blk0 = row0 // rows
    n_halo = state_pad.shape[1]
    n_keys = WINDOW + dec_seq
    bias = _attn_bias(dec_seq, n_keys, 0)
    return pl.pallas_call(
        _attn_sample_kernel,
        grid=(dec_batch // ns,),
        in_specs=[
            pl.BlockSpec(memory_space=pltpu.SMEM),
            pl.BlockSpec((N_KV_HEADS, GROUP * dec_seq, n_keys), lambda b: (0, 0, 0)),
            pl.BlockSpec((rows, q_w), lambda b: (blk0 + b, 0)),
            pl.BlockSpec((rows, kv_w), lambda b: (blk0 + b, kcol)),
            pl.BlockSpec((rows, kv_w), lambda b: (blk0 + b, vcol)),
            pl.BlockSpec((rows, u_w), lambda b: (blk0 + b, ucol)),
            pl.BlockSpec((ns, WINDOW, kv_w), lambda b: (b, 0, 0)),
            pl.BlockSpec((ns, WINDOW, kv_w), lambda b: (b, 0, 0)),
            pl.BlockSpec((ns, n_halo, u_w), lambda b: (b, 0, 0)),
            pl.BlockSpec(memory_space=pl.ANY),
            pl.BlockSpec(memory_space=pl.ANY),
        ],
        out_specs=[
            pl.BlockSpec((rows, q_w), lambda b: (blk0 + b, 0)),
            pl.BlockSpec((rows, u_w), lambda b: (blk0 + b, 0)),
        ],
        out_shape=[
            jax.ShapeDtypeStruct(o_attn.shape, o_attn.dtype),
            jax.ShapeDtypeStruct(z.shape, z.dtype),
        ],
        input_output_aliases={9: 0, 10: 1},
        compiler_params=_cparams(("arbitrary",)),
        name="attn_sample",
    )(sinks, bias, p_all, p_all, p_all, p_all, cache_k, cache_v, state_pad, o_attn, z)


def _finish_kernel(xp_ref, xs_ref, oa_ref, z_ref, ag_ref, pg_ref, wpm_ref, ps_ref, wba_ref, wbp_ref,
                   wo_ref, gf_ref, wr_ref, br_ref,
                   x1_ref, h2_ref, idx_ref, gate_ref, rank_ref, cnt_ref, carry_ref,
                   *, n_prompt_tiles):
    tm = xp_ref.shape[0]
    i = pl.program_id(0)

    @pl.when(i == 0)
    def _():
        carry_ref[...] = jnp.zeros_like(carry_ref)

    x_in = jnp.where(i < n_prompt_tiles, xp_ref[...], xs_ref[...])

    z = z_ref[...]
    zp = jnp.concatenate(
        [jnp.dot(z[:, g * POOL_GROUP_DIM:(g + 1) * POOL_GROUP_DIM], wpm_ref[g],
                 preferred_element_type=F32) for g in range(len(POOL_WINDOWS))], axis=1)
    zp = (zp * ps_ref[...]).astype(BF16)
    ya = jnp.dot(oa_ref[...], wba_ref[...], preferred_element_type=F32)
    yp = jnp.dot(zp, wbp_ref[...], preferred_element_type=F32)
    merged = jax.nn.sigmoid(ag_ref[...]) * ya + jax.nn.sigmoid(pg_ref[...]) * yp
    x1 = x_in + jnp.dot(merged.astype(BF16), wo_ref[...], preferred_element_type=F32)
    x1_ref[...] = x1
    h2 = _rmsnorm(x1, gf_ref[...])
    n_lc = D_MODEL // LANES
    for c in range(n_lc):
        h2_ref[pl.ds(c, tm, stride=n_lc), :] = h2[:, c * LANES:(c + 1) * LANES]

    h_hi = h2.astype(BF16)
    h_lo = (h2 - h_hi.astype(F32)).astype(BF16)
    t = jnp.dot(h_hi, wr_ref[...], preferred_element_type=F32)
    logits = (t[:, :N_EXPERTS] + t[:, N_EXPERTS:]
              + jnp.dot(h_lo, wr_ref[:, :N_EXPERTS], preferred_element_type=F32) + br_ref[...])

    col = lax.broadcasted_iota(jnp.int32, logits.shape, 1).astype(F32)
    vals, idxs = [], []
    for _ in range(TOP_K):
        m = jnp.max(logits, axis=-1, keepdims=True)
        idx = jnp.min(jnp.where(logits == m, col, float(N_EXPERTS)), axis=-1, keepdims=True)
        vals.append(m)
        idxs.append(idx)
        logits = jnp.where(col == idx, -jnp.inf, logits)
    exps = [jnp.exp(v - vals[0]) for v in vals]
    den = exps[0] + exps[1] + exps[2] + exps[3]

    member = jnp.zeros(logits.shape, F32)
    for k in range(TOP_K):
        member = member + (col == idxs[k]).astype(F32)
    ri = lax.broadcasted_iota(jnp.int32, (tm, tm), 0)
    ci = lax.broadcasted_iota(jnp.int32, (tm, tm), 1)
    earlier = (ri > ci).astype(BF16)
    before = jnp.dot(earlier, member.astype(BF16), preferred_element_type=F32) + carry_ref[...]
    for k in range(TOP_K):
        idx_ref[:, k:k + 1] = idxs[k].astype(jnp.int32)
        gate_ref[:, k:k + 1] = exps[k] / den
        rank_ref[:, k:k + 1] = jnp.sum(jnp.where(col == idxs[k], before, 0.0), axis=-1,
                                       keepdims=True).astype(jnp.int32)
    carry_ref[...] += jnp.sum(member, axis=0, keepdims=True)
    cnt_ref[...] = carry_ref[...]


def _finish(x_prompt, x_sample, o_attn, z, p_all, wpm, pool_scale, wba, wbp, wo, g_ffn, w_router2,
            b_router):
    n_tok = o_attn.shape[0]
    tm = ROW_TM
    n_prompt_tiles = x_prompt.shape[0] // tm
    n_sample_tiles = x_sample.shape[0] // tm
    acol, pcol = (IN_COLS - 2 * D_MODEL) // D_MODEL, (IN_COLS - D_MODEL) // D_MODEL
    const = pl.Buffered(1)

    def whole(shape):
        nd = len(shape)
        return pl.BlockSpec(shape, lambda i: (0,) * nd, pipeline_mode=const)

    return pl.pallas_call(
        functools.partial(_finish_kernel, n_prompt_tiles=n_prompt_tiles),
        grid=(n_tok // tm,),
        in_specs=[
            pl.BlockSpec((tm, D_MODEL), lambda i: (jnp.minimum(i, n_prompt_tiles - 1), 0)),
            pl.BlockSpec((tm, D_MODEL),
                         lambda i: (jnp.clip(i - n_prompt_tiles, 0, n_sample_tiles - 1), 0)),
            pl.BlockSpec((tm, ATTN_WIDTH), lambda i: (i, 0)),
            pl.BlockSpec((tm, POOL_WIDTH), lambda i: (i, 0)),
            pl.BlockSpec((tm, D_MODEL), lambda i: (i, acol)),
            pl.BlockSpec((tm, D_MODEL), lambda i: (i, pcol)),
            whole(wpm.shape), whole((1, POOL_WIDTH)), whole(wba.shape), whole(wbp.shape),
            whole(wo.shape), whole((1, D_MODEL)), whole(w_router2.shape), whole((1, N_EXPERTS)),
        ],
        out_specs=[
            pl.BlockSpec((tm, D_MODEL), lambda i: (i, 0)),
            pl.BlockSpec((tm * (D_MODEL // LANES), LANES), lambda i: (i, 0)),
            pl.BlockSpec((tm, TOP_K), lambda i: (i, 0)),
            pl.BlockSpec((tm, TOP_K), lambda i: (i, 0)),
            pl.BlockSpec((tm, TOP_K), lambda i: (i, 0)),
            pl.BlockSpec((1, N_EXPERTS), lambda i: (0, 0)),
        ],
        out_shape=[
            jax.ShapeDtypeStruct((n_tok, D_MODEL), F32),
            jax.ShapeDtypeStruct((n_tok * (D_MODEL // LANES), LANES), F32),
            jax.ShapeDtypeStruct((n_tok, TOP_K), jnp.int32),
            jax.ShapeDtypeStruct((n_tok, TOP_K), F32),
            jax.ShapeDtypeStruct((n_tok, TOP_K), jnp.int32),
            jax.ShapeDtypeStruct((1, N_EXPERTS), F32),
        ],
        scratch_shapes=[pltpu.VMEM((1, N_EXPERTS), F32)],
        compiler_params=_cparams(("arbitrary",)),
        name="finish",
    )(x_prompt, x_sample, o_attn, z, p_all, p_all, wpm, pool_scale.reshape(1, POOL_WIDTH), wba, wbp,
      wo, g_ffn.reshape(1, D_MODEL), w_router2, b_router.reshape(1, N_EXPERTS))


def _route(top_idx, rank, counts_f, n_seg, cap):
    counts = counts_f.reshape(N_EXPERTS).astype(jnp.int32)
    padded = (counts + MOE_ROWS - 1) // MOE_ROWS * MOE_ROWS
    pend = jnp.cumsum(padded)
    pstart = pend - padded
    experts = jnp.arange(N_EXPERTS, dtype=jnp.int32)
    start_of = jnp.sum(jnp.where(top_idx[..., None] == experts, pstart, 0), axis=-1)
    dest = (start_of + rank).reshape(-1)
    tok_sorted = jnp.zeros((cap + MOE_GATHER,), jnp.int32).at[dest].set(
        jnp.arange(dest.shape[0], dtype=jnp.int32) // TOP_K)

    nseg_e = (counts + MOE_MAXM - 1) // MOE_MAXM
    seg_end = jnp.cumsum(nseg_e)
    seg_base = seg_end - nseg_e
    s_ids = jnp.arange(n_seg, dtype=jnp.int32)
    last = jnp.maximum(seg_end[-1] - 1, 0)
    s_eff = jnp.minimum(s_ids, last)
    e_of_s = jnp.minimum(jnp.sum(s_eff[:, None] >= seg_end[None, :], axis=1), N_EXPERTS - 1)
    e_prev = jnp.concatenate([e_of_s[:1], e_of_s[:-1]])
    is_last = s_ids >= last
    e_next = jnp.where(is_last, e_of_s, jnp.concatenate([e_of_s[1:], e_of_s[-1:]]))
    c_next = jnp.where(is_last, MOE_NFA - 1, 0)
    k_in = s_eff - seg_base[e_of_s]
    nrows = jnp.clip(counts[e_of_s] - k_in * MOE_MAXM, 0, MOE_MAXM)
    n_gran = (nrows + MOE_ROWS - 1) // MOE_ROWS
    row0 = pstart[e_of_s] + k_in * MOE_MAXM
    total_gran = (pend[-1] // MOE_ROWS).reshape(1)
    i32 = jnp.int32
    n_live = (last + 1).astype(i32)
    return (n_live, e_of_s.astype(i32), e_prev.astype(i32), e_next.astype(i32), c_next.astype(i32),
            row0.astype(i32), n_gran.astype(i32), total_gran.astype(i32), tok_sorted, dest.astype(i32))


def _for_blocks(n_gran, fn):
    n_full = n_gran // 2

    def pair(p, carry):
        r = p * (2 * MOE_SUB)
        fn(r, MOE_SUB)
        fn(r + MOE_SUB, MOE_SUB)
        return carry
    lax.fori_loop(0, n_full // 2, pair, 0)

    @pl.when(n_full % 2 == 1)
    def _():
        fn((n_full - 1) * MOE_SUB, MOE_SUB)

    @pl.when(n_gran % 2 == 1)
    def _():
        fn(n_full * MOE_SUB, MOE_ROWS)


def _moe_kernel(seg_e, seg_ep, seg_en, seg_cn, seg_row0, seg_ngran, total_gran, tok_smem,
                h2_hbm, wg_ref, wl_ref, wd_ref, bg_ref, bl_ref, bd_ref,
                y_hbm,
                xbuf, actbuf, stage, wab, wdb, ostage, pend, gsem, osem):
    del seg_e, seg_ep, seg_en, seg_cn
    s = pl.program_id(0)
    j = pl.program_id(1)
    n_gran = seg_ngran[s]
    row0 = seg_row0[s]
    live = n_gran > 0
    gran = MOE_ROWS
    fa, fb = MOE_FA, MOE_FB
    n_lc = D_MODEL // LANES

    def rows_at(first, n_rows):
        return pl.ds(pl.multiple_of(first, MOE_ROWS), n_rows)

    gch = MOE_GATHER

    def issue(seg_row0_, chunk, slot):
        base = seg_row0_ + chunk * gch

        def body(r, carry):
            tok = tok_smem[base + r]
            pltpu.make_async_copy(
                h2_hbm.at[pl.ds(pl.multiple_of(tok * n_lc, n_lc), n_lc)],
                stage.at[slot, pl.ds(pl.multiple_of(r * STAGE_PITCH, SUBLANES), n_lc)],
                gsem.at[slot]).start(priority=1)
            return carry
        lax.fori_loop(0, gch, body, 0, unroll=8)

    def land(chunk, slot):
        pltpu.make_async_copy(h2_hbm.at[pl.ds(0, gch * n_lc)], stage.at[slot, pl.ds(0, gch * n_lc)],
                              gsem.at[slot]).wait()
        rows = rows_at(chunk * gch, gch)
        for lc in range(n_lc):
            xbuf[rows, lc * LANES:(lc + 1) * LANES] = (
                stage[slot, pl.ds(lc, gch, stride=STAGE_PITCH), :].astype(BF16))

    def n_chunks(n_gran_):
        return (n_gran_ * gran + gch - 1) // gch

    @pl.when(jnp.logical_and(s == 0, j == 0))
    def _first():
        pend[0] = 0
        pend[1] = 0

        nc = n_chunks(n_gran)
        issue(row0, 0, 0)

        def chunk_body(c, carry):
            slot = c % 2

            @pl.when(c + 1 < nc)
            def _():
                issue(row0, c + 1, 1 - slot)
            land(c, slot)
            return carry
        lax.fori_loop(0, nc, chunk_body, 0)

    def prefetch_next_rows(jb):
        has_next = s + 1 < pl.num_programs(0)
        s_next = jnp.minimum(s + 1, pl.num_programs(0) - 1)
        nc = jnp.where(has_next, n_chunks(seg_ngran[s_next]), 0)
        next_row0 = seg_row0[s_next]
        for q in range(2):
            @pl.when(jnp.logical_and(jb >= 1, 2 * (jb - 1) + q < nc))
            def _():
                land(2 * (jb - 1) + q, q)
        for q in range(2):
            @pl.when(jnp.logical_and(jb + 1 < MOE_NFB, 2 * jb + q < nc))
            def _():
                issue(next_row0, 2 * jb + q, q)

    @pl.when(jnp.logical_and(j < MOE_NFA, live))
    def _up():
        wab[:, :fa] = wg_ref[0].astype(BF16)
        wab[:, fa:] = wl_ref[0].astype(BF16)
        bias = jnp.concatenate([bg_ref[0], bl_ref[0]], axis=1)

        def one(first, n_rows):
            rows = rows_at(first, n_rows)
            gu = jnp.dot(xbuf[rows, :], wab[...], preferred_element_type=F32) + bias
            glu = jnp.minimum(gu[:, :fa], SWIGLU_LIMIT)
            lin = jnp.clip(gu[:, fa:], -SWIGLU_LIMIT, SWIGLU_LIMIT)
            actbuf[j, rows, :] = (glu * jax.nn.sigmoid(SWIGLU_ALPHA * glu) * (lin + 1.0)).astype(BF16)
        _for_blocks(n_gran, one)

    def out_copy(slot, src_row, dst_row, col0):
        return pltpu.make_async_copy(ostage.at[slot, rows_at(src_row, gran)],
                                     y_hbm.at[rows_at(dst_row, gran), pl.ds(col0, fb)], osem.at[slot])

    def out_wait(slot):
        def body(i, carry):
            out_copy(slot, 0, 0, 0).wait()
            return carry
        lax.fori_loop(0, pend[slot], body, 0)
        pend[slot] = 0

    @pl.when(jnp.logical_and(j >= MOE_NFA, live))
    def _down():
        jb = j - MOE_NFA
        slot = jb % 2
        col0 = pl.multiple_of(jb * fb, fb)
        prefetch_next_rows(jb)
        wdb[...] = wd_ref[0].astype(BF16)
        out_wait(slot)

        def one(first, n_rows):
            rows = rows_at(first, n_rows)
            act = jnp.concatenate([actbuf[c, rows, :] for c in range(MOE_NFA)], axis=1)
            ostage[slot, rows, :] = jnp.dot(act, wdb[...], preferred_element_type=F32) + bd_ref[0]
            for h in range(n_rows // gran):
                out_copy(slot, first + h * gran, row0 + first + h * gran, col0).start()
        _for_blocks(n_gran, one)
        pend[slot] = n_gran

    @pl.when(jnp.logical_and(s == pl.num_programs(0) - 1, j == pl.num_programs(1) - 1))
    def _final():
        out_wait(0)
        out_wait(1)
        n_tail = y_hbm.shape[0] // gran - total_gran[0]

        @pl.when(n_tail > 0)
        def _():
            ostage[0, pl.ds(0, gran), :] = jnp.zeros((gran, fb), F32)

            def fill(t, carry):
                for cb in range(MOE_NFB):
                    out_copy(0, 0, (total_gran[0] + t) * gran, cb * fb).start()
                return carry
            lax.fori_loop(0, n_tail, fill, 0)
            pend[0] = n_tail * MOE_NFB
            out_wait(0)


def _moe(h2, n_live, seg_e, seg_ep, seg_en, seg_cn, seg_row0, seg_ngran, total_gran, tok_sorted,
         w_gate_up, b_gate_up, w_down, b_down, cap):
    nfa, nfb = MOE_NFA, MOE_NFB
    assert MOE_MAXM <= 2 * (nfb - 1) * MOE_GATHER

    def up_block(s, j, e, en, cn):
        ahead = j >= nfa + nfb // 2
        eb = jnp.where(ahead, en[s], e[s])
        cb = jnp.where(j < nfa, j, jnp.where(ahead, cn[s], nfa - 1))
        return eb, cb

    def wg_map(s, j, e, ep, en, cn, r0, ng, tg, tok):
        eb, cb = up_block(s, j, e, en, cn)
        return (eb, 0, cb)

    def wl_map(s, j, e, ep, en, cn, r0, ng, tg, tok):
        eb, cb = up_block(s, j, e, en, cn)
        return (eb, 0, nfa + cb)

    def wd_map(s, j, e, ep, en, cn, r0, ng, tg, tok):
        in_up = j < nfa
        eb = jnp.where(in_up, ep[s], e[s])
        jb = jnp.where(in_up, jnp.where(s > 0, nfb - 1, 0), j - nfa)
        return (eb, 0, jb)

    grid_spec = pltpu.PrefetchScalarGridSpec(
        num_scalar_prefetch=8,
        grid=(n_live, nfa + nfb),
        in_specs=[
            pl.BlockSpec(memory_space=pl.ANY),
            pl.BlockSpec((1, D_MODEL, MOE_FA), wg_map),
            pl.BlockSpec((1, D_MODEL, MOE_FA), wl_map),
            pl.BlockSpec((1, D_FF, MOE_FB), wd_map),
            pl.BlockSpec((1, 1, MOE_FA), wg_map),
            pl.BlockSpec((1, 1, MOE_FA), wl_map),
            pl.BlockSpec((1, 1, MOE_FB), wd_map),
        ],
        out_specs=pl.BlockSpec(memory_space=pl.ANY),
        scratch_shapes=[
            pltpu.VMEM((MOE_MAXM, D_MODEL), BF16),
            pltpu.VMEM((nfa, MOE_MAXM, MOE_FA), BF16),
            pltpu.VMEM((2, MOE_GATHER * STAGE_PITCH, LANES), F32),
            pltpu.VMEM((D_MODEL, 2 * MOE_FA), BF16),
            pltpu.VMEM((D_FF, MOE_FB), BF16),
            pltpu.VMEM((2, MOE_MAXM, MOE_FB), F32),
            pltpu.SMEM((2,), jnp.int32),
            pltpu.SemaphoreType.DMA((2,)),
            pltpu.SemaphoreType.DMA((2,)),
        ],
    )
    return pl.pallas_call(
        _moe_kernel,
        grid_spec=grid_spec,
        out_shape=jax.ShapeDtypeStruct((cap, D_MODEL), F32),
        compiler_params=_cparams(("arbitrary", "arbitrary")),
        name="moe",
    )(seg_e, seg_ep, seg_en, seg_cn, seg_row0, seg_ngran, total_gran, tok_sorted,
      h2, w_gate_up, w_gate_up, w_down,
      b_gate_up.reshape(N_EXPERTS, 1, 2 * D_FF), b_gate_up.reshape(N_EXPERTS, 1, 2 * D_FF),
      b_down.reshape(N_EXPERTS, 1, D_MODEL))


def _combine_kernel(pos_ref, y_hbm, x1_ref, gate_ref, gfin_ref, op_ref, os_ref, buf, sem,
                    *, n_prompt_tiles):
    i = pl.program_id(0)
    n = pl.num_programs(0)
    tm = ROW_TM

    def row_copy(tile, slot, r, k):
        p = pos_ref[(tile * tm + r) * TOP_K + k]
        return pltpu.make_async_copy(y_hbm.at[pl.ds(p, 1)], buf.at[slot, k, pl.ds(r, 1)],
                                     sem.at[slot])

    def issue(tile, slot):
        def body(r, carry):
            for k in range(TOP_K):
                row_copy(tile, slot, r, k).start()
            return carry
        lax.fori_loop(0, tm, body, 0, unroll=4)

    def wait(slot):
        for k in range(TOP_K):
            pltpu.make_async_copy(y_hbm.at[pl.ds(0, tm)], buf.at[slot, k], sem.at[slot]).wait()

    slot = i % 2

    @pl.when(i == 0)
    def _():
        issue(0, 0)

    @pl.when(i + 1 < n)
    def _():
        issue(i + 1, 1 - slot)

    wait(slot)
    gate = gate_ref[...]
    x2 = x1_ref[...]
    for k in range(TOP_K):
        x2 = x2 + gate[:, k:k + 1] * buf[slot, k]
    ms = jnp.mean(x2 * x2, axis=-1, keepdims=True)
    y = x2 * lax.rsqrt(ms + EPS) * gfin_ref[...]

    @pl.when(i < n_prompt_tiles)
    def _():
        op_ref[...] = y

    @pl.when(i >= n_prompt_tiles)
    def _():
        os_ref[...] = y


def _combine(pos, y_sorted, x1, gate, g_final, n_prompt, n_sample):
    tm = ROW_TM
    n_tok = x1.shape[0]
    n_prompt_tiles = n_prompt // tm
    n_sample_tiles = n_sample // tm

    grid_spec = pltpu.PrefetchScalarGridSpec(
        num_scalar_prefetch=1,
        grid=(n_tok // tm,),
        in_specs=[
            pl.BlockSpec(memory_space=pl.ANY),
            pl.BlockSpec((tm, D_MODEL), lambda i, pos: (i, 0)),
            pl.BlockSpec((tm, TOP_K), lambda i, pos: (i, 0)),
            pl.BlockSpec((1, D_MODEL), lambda i, pos: (0, 0)),
        ],
        out_specs=[
            pl.BlockSpec((tm, D_MODEL), lambda i, pos: (jnp.minimum(i, n_prompt_tiles - 1), 0)),
            pl.BlockSpec((tm, D_MODEL),
                         lambda i, pos: (jnp.clip(i - n_prompt_tiles, 0, n_sample_tiles - 1), 0)),
        ],
        scratch_shapes=[
            pltpu.VMEM((2, TOP_K, tm, D_MODEL), F32),
            pltpu.SemaphoreType.DMA((2,)),
        ],
    )
    return pl.pallas_call(
        functools.partial(_combine_kernel, n_prompt_tiles=n_prompt_tiles),
        grid_spec=grid_spec,
        out_shape=[
            jax.ShapeDtypeStruct((n_prompt, D_MODEL), F32),
            jax.ShapeDtypeStruct((n_sample, D_MODEL), F32),
        ],
        compiler_params=_cparams(("arbitrary",)),
        name="combine",
    )(pos, y_sorted, x1, gate, g_final.reshape(1, D_MODEL))


def kernel(x_prompt, x_sample, cache_k, cache_v, state_pool, meta_tokens, g_mix, w_in, sinks,
           w_pool_mix, pool_scale, w_br_attn, w_br_pool, w_out, g_ffn, w_router, b_router,
           w_gate_up, b_gate_up, w_down, b_down, g_final):
    depth = w_in.shape[0]
    assert depth == 1, "single-layer step only"
    batch, seq, _ = x_prompt.shape
    dec_batch, dec_seq, _ = x_sample.shape
    n_prompt = batch * seq
    n_sample = dec_batch * dec_seq
    n_tok = n_prompt + n_sample
    assert seq % ATTN_TILE == 0 and n_prompt % ROW_TM == 0 and n_sample % ROW_TM == 0
    assert dec_seq == SUBLANES and dec_batch % SAMPLE_SEQS == 0

    assert n_prompt % PROJ_TM == 0 and n_sample + ATTN_TILE <= PROJ_TM
    xp = x_prompt.reshape(n_prompt, D_MODEL)
    xs = x_sample.reshape(n_sample, D_MODEL)
    pad = jnp.zeros((PROJ_TM - n_sample - N_META, D_MODEL), F32)
    x_tail = jnp.concatenate([xs, pad, meta_tokens.astype(F32)], axis=0)

    p_all = _inproj(_prenorm(xp, x_tail, g_mix[0]), w_in[0])

    ck = cache_k[0].reshape(dec_batch, WINDOW, KV_WIDTH)
    cv = cache_v[0].reshape(dec_batch, WINDOW, KV_WIDTH)
    state_pad = jnp.pad(state_pool[0], ((0, 0), (N_META - POOL_BUF, 0), (0, 0)))
    o_attn, z = _attn_prompt(p_all, sinks[0], batch, seq, n_tok)
    o_attn, z = _attn_sample(p_all, sinks[0], ck, cv, state_pad, o_attn, z, n_prompt, dec_batch,
                             dec_seq)

    wr = w_router[0]
    wr_hi = wr.astype(BF16)
    wr_lo = (wr - wr_hi.astype(F32)).astype(BF16)
    x1, h2, top_idx, gate, rank, counts = _finish(
        xp, xs, o_attn, z, p_all, w_pool_mix[0].astype(BF16), pool_scale[0],
        w_br_attn[0].astype(BF16), w_br_pool[0].astype(BF16), w_out[0].astype(BF16),
        g_ffn[0], jnp.concatenate([wr_hi, wr_lo], axis=1), b_router[0])

    n_assign = n_tok * TOP_K
    n_seg = N_EXPERTS + n_assign // MOE_MAXM
    cap = (n_assign // MOE_ROWS + N_EXPERTS) * MOE_ROWS
    n_live, seg_e, seg_ep, seg_en, seg_cn, seg_row0, seg_ngran, total_gran, tok_sorted, dest = _route(
        top_idx, rank, counts, n_seg, cap)
    y_sorted = _moe(h2, n_live, seg_e, seg_ep, seg_en, seg_cn, seg_row0, seg_ngran, total_gran,
                    tok_sorted, w_gate_up[0], b_gate_up[0], w_down[0], b_down[0], cap)
    y_p, y_s = _combine(dest, y_sorted, x1, gate, g_final, n_prompt, n_sample)

    k0, v0, u0, u1 = ATTN_WIDTH, ATTN_WIDTH + KV_WIDTH, ATTN_WIDTH + 2 * KV_WIDTH, IN_COLS - 2 * D_MODEL

    def tail_rows(n, c0, c1):
        return jnp.stack([p_all[(b + 1) * seq - n:(b + 1) * seq, c0:c1] for b in range(batch)])

    new_k_p = tail_rows(WINDOW, k0, v0).reshape(1, batch, WINDOW, N_KV_HEADS, HEAD_DIM)
    new_v_p = tail_rows(WINDOW, v0, u0).reshape(1, batch, WINDOW, N_KV_HEADS, HEAD_DIM)
    new_u_p = tail_rows(POOL_BUF, u0, u1)[None]
    ps = p_all[n_prompt:n_tok, k0:u1].reshape(dec_batch, dec_seq, u1 - k0)
    new_k_s = jnp.concatenate([ck[:, dec_seq:], ps[:, :, :KV_WIDTH]], axis=1).reshape(
        1, dec_batch, WINDOW, N_KV_HEADS, HEAD_DIM)
    new_v_s = jnp.concatenate([cv[:, dec_seq:], ps[:, :, KV_WIDTH:2 * KV_WIDTH]], axis=1).reshape(
        1, dec_batch, WINDOW, N_KV_HEADS, HEAD_DIM)
    new_u_s = jnp.concatenate([state_pool[0][:, dec_seq:], ps[:, :, 2 * KV_WIDTH:]], axis=1)[None]

    return (y_p.reshape(batch, seq, D_MODEL), y_s.reshape(dec_batch, dec_seq, D_MODEL),
            new_k_p, new_v_p, new_u_p, new_k_s, new_v_s, new_u_s)
```

```python
import functools

import jax
import jax.numpy as jnp
from jax import lax
from jax.experimental import pallas as pl
from jax.experimental.pallas import tpu as pltpu

F32 = jnp.float32
BF16 = jnp.bfloat16

D_MODEL = 2048
N_META = 16
N_HEADS = 32
N_KV_HEADS = 8
HEAD_DIM = 64
GROUP = N_HEADS // N_KV_HEADS
WINDOW = 128
ATTN_WIDTH = N_HEADS * HEAD_DIM
KV_WIDTH = N_KV_HEADS * HEAD_DIM
POOL_WIDTH = D_MODEL // 2
POOL_WINDOWS = (2, 4, 8, 16)
POOL_GROUP_DIM = POOL_WIDTH // len(POOL_WINDOWS)
POOL_BUF = max(POOL_WINDOWS) - 1
N_EXPERTS = 32
TOP_K = 4
D_FF = D_MODEL
SWIGLU_LIMIT = 7.0
SWIGLU_ALPHA = 1.702
EPS = 1e-5
IN_COLS = ATTN_WIDTH + 2 * KV_WIDTH + POOL_WIDTH + 2 * D_MODEL

LANES = 128
SUBLANES = 8
VMEM_LIMIT_BYTES = 56 * 1024 * 1024

ATTN_TILE = WINDOW
PROJ_TM = 512
PROJ_TN = 1024
ROW_TM = 256
SAMPLE_SEQS = 4
MOE_ROWS = 128
MOE_SUB = 2 * MOE_ROWS
MOE_MAXM = 12 * MOE_ROWS
MOE_FA = 512
MOE_FB = 512
MOE_GATHER = 2 * MOE_ROWS
MOE_NFA = D_FF // MOE_FA
MOE_NFB = D_MODEL // MOE_FB
STAGE_PITCH = 3 * SUBLANES
NEG_BIG = -1e30

_SLOPES = tuple(float(2.0 ** (-8.0 * (i + 1) / N_HEADS)) for i in range(N_HEADS))


def _cparams(sem, vmem=VMEM_LIMIT_BYTES):
    return pltpu.CompilerParams(dimension_semantics=sem, vmem_limit_bytes=vmem)


def _rmsnorm(x, g):
    ms = jnp.mean(x * x, axis=-1, keepdims=True)
    return x * lax.rsqrt(ms + EPS) * g


def _prenorm_kernel(xp_ref, xt_ref, g_ref, h_ref, *, n_main):
    x = jnp.where(pl.program_id(0) < n_main, xp_ref[...], xt_ref[...])
    h_ref[...] = _rmsnorm(x, g_ref[...]).astype(h_ref.dtype)


def _prenorm(x_main, x_tail, g_mix):
    n_main = x_main.shape[0] // PROJ_TM
    n_rows = x_main.shape[0] + x_tail.shape[0]
    return pl.pallas_call(
        functools.partial(_prenorm_kernel, n_main=n_main),
        grid=(n_main + 1,),
        in_specs=[
            pl.BlockSpec((PROJ_TM, D_MODEL), lambda m: (jnp.minimum(m, n_main - 1), 0)),
            pl.BlockSpec((PROJ_TM, D_MODEL), lambda m: (0, 0)),
            pl.BlockSpec((1, D_MODEL), lambda m: (0, 0)),
        ],
        out_specs=pl.BlockSpec((PROJ_TM, D_MODEL), lambda m: (m, 0)),
        out_shape=jax.ShapeDtypeStruct((n_rows, D_MODEL), BF16),
        compiler_params=_cparams(("arbitrary",)),
        name="prenorm",
    )(x_main, x_tail, g_mix.reshape(1, D_MODEL))


def _inproj_kernel(h_ref, w_ref, o_ref, wbf_ref):
    @pl.when(pl.program_id(1) == 0)
    def _():
        wbf_ref[...] = w_ref[...].astype(BF16)

    o_ref[...] = jnp.dot(h_ref[...], wbf_ref[...], preferred_element_type=F32)


def _inproj(h, w_in):
    n_rows = h.shape[0]
    grid = (IN_COLS // PROJ_TN, n_rows // PROJ_TM)
    return pl.pallas_call(
        _inproj_kernel,
        grid=grid,
        in_specs=[
            pl.BlockSpec((PROJ_TM, D_MODEL), lambda n, m: (m, 0)),
            pl.BlockSpec((D_MODEL, PROJ_TN), lambda n, m: (0, n)),
        ],
        out_specs=pl.BlockSpec((PROJ_TM, PROJ_TN), lambda n, m: (m, n)),
        out_shape=jax.ShapeDtypeStruct((n_rows, IN_COLS), F32),
        scratch_shapes=[pltpu.VMEM((D_MODEL, PROJ_TN), BF16)],
        compiler_params=_cparams(("arbitrary", "arbitrary")),
        name="inproj",
    )(h, w_in)


def _attn_bias(n_q, n_keys, first_key):
    r = jnp.arange(n_q, dtype=jnp.int32)[:, None]
    c = jnp.arange(n_keys, dtype=jnp.int32)[None, :]
    dist = r + WINDOW - c
    valid = (dist >= 0) & (dist <= WINDOW) & (c >= first_key)
    slopes = jnp.asarray(_SLOPES, F32).reshape(N_KV_HEADS, GROUP, 1, 1)
    bias = jnp.where(valid[None, None], -slopes * dist.astype(F32)[None, None], NEG_BIG)
    return bias.reshape(N_KV_HEADS, GROUP * n_q, n_keys)


def _group_sinks(sink_ref, kvh, n_q):
    g_row = lax.broadcasted_iota(jnp.int32, (GROUP * n_q, 1), 0) // n_q
    sink = jnp.zeros((GROUP * n_q, 1), F32)
    for g in range(GROUP):
        sink = jnp.where(g_row == g, sink_ref[kvh * GROUP + g], sink)
    return sink


def _group_attention(q, k, v, bias, sink_ref, kvh, n_q):
    kh = k[:, kvh * HEAD_DIM:(kvh + 1) * HEAD_DIM]
    vh = v[:, kvh * HEAD_DIM:(kvh + 1) * HEAD_DIM]
    heads = [kvh * GROUP + g for g in range(GROUP)]
    qg = jnp.concatenate([q[:, hd * HEAD_DIM:(hd + 1) * HEAD_DIM] for hd in heads], axis=0)
    s = lax.dot_general(qg, kh, (((1,), (1,)), ((), ())), preferred_element_type=F32) + bias
    sink = _group_sinks(sink_ref, kvh, n_q)
    m = jnp.maximum(jnp.max(s, axis=-1, keepdims=True), sink)
    p = jnp.exp(s - m)
    den = jnp.sum(p, axis=-1, keepdims=True) + jnp.exp(sink - m)
    o = jnp.dot(p.astype(BF16), vh, preferred_element_type=F32)
    return o / den


def _pool_features(ext, n_halo):
    outs = []
    for g, w in enumerate(POOL_WINDOWS):
        e = ext[:, g * POOL_GROUP_DIM:(g + 1) * POOL_GROUP_DIM]
        s = e
        shift = 1
        while shift < w:
            s = s + pltpu.roll(s, shift, axis=0)
            shift *= 2
        outs.append(s[n_halo:] * (1.0 / w) - e[n_halo:])
    return jnp.concatenate(outs, axis=1)


def _attn_prompt_kernel(sink_ref, bias_ref, q_ref, ko_ref, kp_ref, vo_ref, vp_ref, uo_ref, up_ref,
                        o_ref, z_ref, s_scr, p_scr, den_scr, *, n_prompt_tiles):
    t = ATTN_TILE
    step = pl.program_id(0)

    @pl.when(step < n_prompt_tiles)
    def _():
        q = (q_ref[...] * (HEAD_DIM ** -0.5)).astype(BF16)
        k = jnp.concatenate([kp_ref[...], ko_ref[...]], axis=0).astype(BF16)
        v = jnp.concatenate([vp_ref[...], vo_ref[...]], axis=0).astype(BF16)
        for kvh in range(N_KV_HEADS):
            kh = k[:, kvh * HEAD_DIM:(kvh + 1) * HEAD_DIM]
            qg = jnp.concatenate([q[:, (kvh * GROUP + g) * HEAD_DIM:(kvh * GROUP + g + 1) * HEAD_DIM]
                                  for g in range(GROUP)], axis=0)
            s_scr[kvh] = lax.dot_general(qg, kh, (((1,), (1,)), ((), ())),
                                         preferred_element_type=F32) + bias_ref[0, kvh]
        for kvh in range(N_KV_HEADS):
            s = s_scr[kvh]
            sink = _group_sinks(sink_ref, kvh, t)
            m = jnp.maximum(jnp.max(s, axis=-1, keepdims=True), sink)
            p = jnp.exp(s - m)
            den_scr[kvh] = jnp.sum(p, axis=-1, keepdims=True) + jnp.exp(sink - m)
            p_scr[kvh] = p.astype(BF16)
        for kvh in range(N_KV_HEADS):
            vh = v[:, kvh * HEAD_DIM:(kvh + 1) * HEAD_DIM]
            o = jnp.dot(p_scr[kvh], vh, preferred_element_type=F32) / den_scr[kvh]
            for g in range(GROUP):
                hd = kvh * GROUP + g
                o_ref[:, hd * HEAD_DIM:(hd + 1) * HEAD_DIM] = o[g * t:(g + 1) * t].astype(o_ref.dtype)

        ext = jnp.concatenate([up_ref[...], uo_ref[...]], axis=0)
        z_ref[...] = _pool_features(ext, N_META).astype(z_ref.dtype)

    @pl.when(step >= n_prompt_tiles)
    def _():
        o_ref[...] = jnp.zeros_like(o_ref)
        z_ref[...] = jnp.zeros_like(z_ref)


def _attn_prompt(p_all, sinks, batch, seq, n_out_rows):
    t = ATTN_TILE
    tiles = seq // t
    n_prompt_tiles = batch * tiles
    assert n_out_rows % t == 0
    meta_blk = p_all.shape[0] // t - 1
    q_w, kv_w, u_w = ATTN_WIDTH, KV_WIDTH, POOL_WIDTH
    kcol, vcol, ucol = ATTN_WIDTH // kv_w, ATTN_WIDTH // kv_w + 1, (ATTN_WIDTH + 2 * kv_w) // u_w
    halo_per_tile = t // N_META
    bias = jnp.stack([_attn_bias(t, 2 * t, t - N_META), _attn_bias(t, 2 * t, 0)])

    def prev(s):
        return jnp.where(s % tiles > 0, s - 1, meta_blk)

    return pl.pallas_call(
        functools.partial(_attn_prompt_kernel, n_prompt_tiles=n_prompt_tiles),
        grid=(n_out_rows // t,),
        in_specs=[
            pl.BlockSpec(memory_space=pltpu.SMEM),
            pl.BlockSpec((1, N_KV_HEADS, GROUP * t, 2 * t), lambda s: (jnp.minimum(s % tiles, 1), 0, 0, 0)),
            pl.BlockSpec((t, q_w), lambda s: (s, 0)),
            pl.BlockSpec((t, kv_w), lambda s: (s, kcol)),
            pl.BlockSpec((t, kv_w), lambda s: (prev(s), kcol)),
            pl.BlockSpec((t, kv_w), lambda s: (s, vcol)),
            pl.BlockSpec((t, kv_w), lambda s: (prev(s), vcol)),
            pl.BlockSpec((t, u_w), lambda s: (s, ucol)),
            pl.BlockSpec((N_META, u_w), lambda s: (prev(s) * halo_per_tile + halo_per_tile - 1, ucol)),
        ],
        out_specs=[
            pl.BlockSpec((t, q_w), lambda s: (s, 0)),
            pl.BlockSpec((t, u_w), lambda s: (s, 0)),
        ],
        out_shape=[
            jax.ShapeDtypeStruct((n_out_rows, q_w), BF16),
            jax.ShapeDtypeStruct((n_out_rows, u_w), BF16),
        ],
        scratch_shapes=[
            pltpu.VMEM((N_KV_HEADS, GROUP * t, 2 * t), F32),
            pltpu.VMEM((N_KV_HEADS, GROUP * t, 2 * t), BF16),
            pltpu.VMEM((N_KV_HEADS, GROUP * t, 1), F32),
        ],
        compiler_params=_cparams(("arbitrary",)),
        name="attn_prompt",
    )(sinks, bias, p_all, p_all, p_all, p_all, p_all, p_all, p_all)


def _attn_sample_kernel(sink_ref, bias_ref, q_ref, kn_ref, vn_ref, un_ref, ck_ref, cv_ref, sp_ref,
                        o_prev_ref, z_prev_ref, o_ref, z_ref):
    del o_prev_ref, z_prev_ref
    n_seqs = ck_ref.shape[0]
    n_new = q_ref.shape[0] // n_seqs
    q_all = (q_ref[...] * (HEAD_DIM ** -0.5)).astype(BF16)
    for si in range(n_seqs):
        rows = slice(si * n_new, (si + 1) * n_new)
        q = q_all[rows]
        k = jnp.concatenate([ck_ref[si], kn_ref[rows, :]], axis=0).astype(BF16)
        v = jnp.concatenate([cv_ref[si], vn_ref[rows, :]], axis=0).astype(BF16)
        for kvh in range(N_KV_HEADS):
            o = _group_attention(q, k, v, bias_ref[kvh], sink_ref, kvh, n_new)
            for g in range(GROUP):
                hd = kvh * GROUP + g
                o_ref[rows, hd * HEAD_DIM:(hd + 1) * HEAD_DIM] = (
                    o[g * n_new:(g + 1) * n_new].astype(o_ref.dtype))
        ext = jnp.concatenate([sp_ref[si], un_ref[rows, :]], axis=0)
        z_ref[rows, :] = _pool_features(ext, sp_ref.shape[1]).astype(z_ref.dtype)


def _attn_sample(p_all, sinks, cache_k, cache_v, state_pad, o_attn, z, row0, dec_batch, dec_seq):
    q_w, kv_w, u_w = ATTN_WIDTH, KV_WIDTH, POOL_WIDTH
    kcol, vcol, ucol = ATTN_WIDTH // kv_w, ATTN_WIDTH // kv_w + 1, (ATTN_WIDTH + 2 * kv_w) // u_w
    ns = SAMPLE_SEQS
    rows = ns * dec_seq
    blk0 = row0 // rows
    n_halo = state_pad.shape[1]
    n_keys = WINDOW + dec_seq
    bias = _attn_bias(dec_seq, n_keys, 0)
    return pl.pallas_call(
        _attn_sample_kernel,
        grid=(dec_batch // ns,),
        in_specs=[
            pl.BlockSpec(memory_space=pltpu.SMEM),
            pl.BlockSpec((N_KV_HEADS, GROUP * dec_seq, n_keys), lambda b: (0, 0, 0)),
            pl.BlockSpec((rows, q_w), lambda b: (blk0 + b, 0)),
            pl.BlockSpec((rows, kv_w), lambda b: (blk0 + b, kcol)),
            pl.BlockSpec((rows, kv_w), lambda b: (blk0 + b, vcol)),
            pl.BlockSpec((rows, u_w), lambda b: (blk0 + b, ucol)),
            pl.BlockSpec((ns, WINDOW, kv_w), lambda b: (b, 0, 0)),
            pl.BlockSpec((ns, WINDOW, kv_w), lambda b: (b, 0, 0)),
            pl.BlockSpec((ns, n_halo, u_w), lambda b: (b, 0, 0)),
            pl.BlockSpec(memory_space=pl.ANY),
            pl.BlockSpec(memory_space=pl.ANY),
        ],
        out_specs=[
            pl.BlockSpec((rows, q_w), lambda b: (blk0 + b, 0)),
            pl.BlockSpec((rows, u_w), lambda b: (blk0 + b, 0)),
        ],
        out_shape=[
            jax.ShapeDtypeStruct(o_attn.shape, o_attn.dtype),
            jax.ShapeDtypeStruct(z.shape, z.dtype),
        ],
        input_output_aliases={9: 0, 10: 1},
        compiler_params=_cparams(("arbitrary",)),
        name="attn_sample",
    )(sinks, bias, p_all, p_all, p_all, p_all, cache_k, cache_v, state_pad, o_attn, z)


def _finish_kernel(xp_ref, xs_ref, oa_ref, z_ref, ag_ref, pg_ref, wpm_ref, ps_ref, wba_ref, wbp_ref,
                   wo_ref, gf_ref, wr_ref, br_ref,
                   x1_ref, h2_ref, idx_ref, gate_ref, rank_ref, cnt_ref, carry_ref,
                   *, n_prompt_tiles):
    tm = xp_ref.shape[0]
    i = pl.program_id(0)

    @pl.when(i == 0)
    def _():
        carry_ref[...] = jnp.zeros_like(carry_ref)

    x_in = jnp.where(i < n_prompt_tiles, xp_ref[...], xs_ref[...])

    z = z_ref[...]
    zp = jnp.concatenate(
        [jnp.dot(z[:, g * POOL_GROUP_DIM:(g + 1) * POOL_GROUP_DIM], wpm_ref[g],
                 preferred_element_type=F32) for g in range(len(POOL_WINDOWS))], axis=1)
    zp = (zp * ps_ref[...]).astype(BF16)
    ya = jnp.dot(oa_ref[...], wba_ref[...], preferred_element_type=F32)
    yp = jnp.dot(zp, wbp_ref[...], preferred_element_type=F32)
    merged = jax.nn.sigmoid(ag_ref[...]) * ya + jax.nn.sigmoid(pg_ref[...]) * yp
    x1 = x_in + jnp.dot(merged.astype(BF16), wo_ref[...], preferred_element_type=F32)
    x1_ref[...] = x1
    h2 = _rmsnorm(x1, gf_ref[...])
    n_lc = D_MODEL // LANES
    for c in range(n_lc):
        h2_ref[pl.ds(c, tm, stride=n_lc), :] = h2[:, c * LANES:(c + 1) * LANES]

    h_hi = h2.astype(BF16)
    h_lo = (h2 - h_hi.astype(F32)).astype(BF16)
    t = jnp.dot(h_hi, wr_ref[...], preferred_element_type=F32)
    logits = (t[:, :N_EXPERTS] + t[:, N_EXPERTS:]
              + jnp.dot(h_lo, wr_ref[:, :N_EXPERTS], preferred_element_type=F32) + br_ref[...])

    col = lax.broadcasted_iota(jnp.int32, logits.shape, 1).astype(F32)
    vals, idxs = [], []
    for _ in range(TOP_K):
        m = jnp.max(logits, axis=-1, keepdims=True)
        idx = jnp.min(jnp.where(logits == m, col, float(N_EXPERTS)), axis=-1, keepdims=True)
        vals.append(m)
        idxs.append(idx)
        logits = jnp.where(col == idx, -jnp.inf, logits)
    exps = [jnp.exp(v - vals[0]) for v in vals]
    den = exps[0] + exps[1] + exps[2] + exps[3]

    member = jnp.zeros(logits.shape, F32)
    for k in range(TOP_K):
        member = member + (col == idxs[k]).astype(F32)
    ri = lax.broadcasted_iota(jnp.int32, (tm, tm), 0)
    ci = lax.broadcasted_iota(jnp.int32, (tm, tm), 1)
    earlier = (ri > ci).astype(BF16)
    before = jnp.dot(earlier, member.astype(BF16), preferred_element_type=F32) + carry_ref[...]
    for k in range(TOP_K):
        idx_ref[:, k:k + 1] = idxs[k].astype(jnp.int32)
        gate_ref[:, k:k + 1] = exps[k] / den
        rank_ref[:, k:k + 1] = jnp.sum(jnp.where(col == idxs[k], before, 0.0), axis=-1,
                                       keepdims=True).astype(jnp.int32)
    carry_ref[...] += jnp.sum(member, axis=0, keepdims=True)
    cnt_ref[...] = carry_ref[...]


def _finish(x_prompt, x_sample, o_attn, z, p_all, wpm, pool_scale, wba, wbp, wo, g_ffn, w_router2,
            b_router):
    n_tok = o_attn.shape[0]
    tm = ROW_TM
    n_prompt_tiles = x_prompt.shape[0] // tm
    n_sample_tiles = x_sample.shape[0] // tm
    acol, pcol = (IN_COLS - 2 * D_MODEL) // D_MODEL, (IN_COLS - D_MODEL) // D_MODEL
    const = pl.Buffered(1)

    def whole(shape):
        nd = len(shape)
        return pl.BlockSpec(shape, lambda i: (0,) * nd, pipeline_mode=const)

    return pl.pallas_call(
        functools.partial(_finish_kernel, n_prompt_tiles=n_prompt_tiles),
        grid=(n_tok // tm,),
        in_specs=[
            pl.BlockSpec((tm, D_MODEL), lambda i: (jnp.minimum(i, n_prompt_tiles - 1), 0)),
            pl.BlockSpec((tm, D_MODEL),
                         lambda i: (jnp.clip(i - n_prompt_tiles, 0, n_sample_tiles - 1), 0)),
            pl.BlockSpec((tm, ATTN_WIDTH), lambda i: (i, 0)),
            pl.BlockSpec((tm, POOL_WIDTH), lambda i: (i, 0)),
            pl.BlockSpec((tm, D_MODEL), lambda i: (i, acol)),
            pl.BlockSpec((tm, D_MODEL), lambda i: (i, pcol)),
            whole(wpm.shape), whole((1, POOL_WIDTH)), whole(wba.shape), whole(wbp.shape),
            whole(wo.shape), whole((1, D_MODEL)), whole(w_router2.shape), whole((1, N_EXPERTS)),
        ],
        out_specs=[
            pl.BlockSpec((tm, D_MODEL), lambda i: (i, 0)),
            pl.BlockSpec((tm * (D_MODEL // LANES), LANES), lambda i: (i, 0)),
            pl.BlockSpec((tm, TOP_K), lambda i: (i, 0)),
            pl.BlockSpec((tm, TOP_K), lambda i: (i, 0)),
            pl.BlockSpec((tm, TOP_K), lambda i: (i, 0)),
            pl.BlockSpec((1, N_EXPERTS), lambda i: (0, 0)),
        ],
        out_shape=[
            jax.ShapeDtypeStruct((n_tok, D_MODEL), F32),
            jax.ShapeDtypeStruct((n_tok * (D_MODEL // LANES), LANES), F32),
            jax.ShapeDtypeStruct((n_tok, TOP_K), jnp.int32),
            jax.ShapeDtypeStruct((n_tok, TOP_K), F32),
            jax.ShapeDtypeStruct((n_tok, TOP_K), jnp.int32),
            jax.ShapeDtypeStruct((1, N_EXPERTS), F32),
        ],
        scratch_shapes=[pltpu.VMEM((1, N_EXPERTS), F32)],
        compiler_params=_cparams(("arbitrary",)),
        name="finish",
    )(x_prompt, x_sample, o_attn, z, p_all, p_all, wpm, pool_scale.reshape(1, POOL_WIDTH), wba, wbp,
      wo, g_ffn.reshape(1, D_MODEL), w_router2, b_router.reshape(1, N_EXPERTS))


def _route(top_idx, rank, counts_f, n_seg):
    counts = counts_f.reshape(N_EXPERTS).astype(jnp.int32)
    padded = (counts + MOE_ROWS - 1) // MOE_ROWS * MOE_ROWS
    pend = jnp.cumsum(padded)
    pstart = pend - padded
    experts = jnp.arange(N_EXPERTS, dtype=jnp.int32)
    start_of = jnp.sum(jnp.where(top_idx[..., None] == experts, pstart, 0), axis=-1)
    dest = (start_of + rank).reshape(-1)
    _, tok_order = lax.sort_key_val(dest, jnp.arange(dest.shape[0], dtype=jnp.int32) // TOP_K)
    tok_sorted = jnp.concatenate([tok_order, jnp.zeros((MOE_GATHER,), jnp.int32)])
    cstart = jnp.cumsum(counts) - counts

    nseg_e = (counts + MOE_MAXM - 1) // MOE_MAXM
    seg_end = jnp.cumsum(nseg_e)
    seg_base = seg_end - nseg_e
    s_ids = jnp.arange(n_seg, dtype=jnp.int32)
    last = jnp.maximum(seg_end[-1] - 1, 0)
    s_eff = jnp.minimum(s_ids, last)
    e_of_s = jnp.minimum(jnp.sum(s_eff[:, None] >= seg_end[None, :], axis=1), N_EXPERTS - 1)
    e_prev = jnp.concatenate([e_of_s[:1], e_of_s[:-1]])
    is_last = s_ids >= last
    e_next = jnp.where(is_last, e_of_s, jnp.concatenate([e_of_s[1:], e_of_s[-1:]]))
    c_next = jnp.where(is_last, MOE_NFA - 1, 0)
    k_in = s_eff - seg_base[e_of_s]
    nrows = jnp.clip(counts[e_of_s] - k_in * MOE_MAXM, 0, MOE_MAXM)
    n_gran = (nrows + MOE_ROWS - 1) // MOE_ROWS
    row0 = pstart[e_of_s] + k_in * MOE_MAXM
    tok0 = cstart[e_of_s] + k_in * MOE_MAXM
    total_gran = (pend[-1] // MOE_ROWS).reshape(1)
    i32 = jnp.int32
    n_live = (last + 1).astype(i32)
    return (n_live, e_of_s.astype(i32), e_prev.astype(i32), e_next.astype(i32), c_next.astype(i32),
            row0.astype(i32), tok0.astype(i32), n_gran.astype(i32), total_gran.astype(i32), tok_sorted,
            dest.astype(i32))


def _for_blocks(n_gran, fn):
    n_full = n_gran // 2

    def pair(p, carry):
        r = p * (2 * MOE_SUB)
        fn(r, MOE_SUB)
        fn(r + MOE_SUB, MOE_SUB)
        return carry
    lax.fori_loop(0, n_full // 2, pair, 0)

    @pl.when(n_full % 2 == 1)
    def _():
        fn((n_full - 1) * MOE_SUB, MOE_SUB)

    @pl.when(n_gran % 2 == 1)
    def _():
        fn(n_full * MOE_SUB, MOE_ROWS)


def _moe_kernel(seg_e, seg_ep, seg_en, seg_cn, seg_row0, seg_tok0, seg_ngran, total_gran, tok_smem,
                h2_hbm, wg_ref, wl_ref, wd_ref, bg_ref, bl_ref, bd_ref,
                y_hbm,
                xbuf, actbuf, stage, wab, wdb, ostage, pend, gsem, osem):
    del seg_e, seg_ep, seg_en, seg_cn
    s = pl.program_id(0)
    j = pl.program_id(1)
    n_gran = seg_ngran[s]
    row0 = seg_row0[s]
    live = n_gran > 0
    gran = MOE_ROWS
    fa, fb = MOE_FA, MOE_FB
    n_lc = D_MODEL // LANES

    def rows_at(first, n_rows):
        return pl.ds(pl.multiple_of(first, MOE_ROWS), n_rows)

    gch = MOE_GATHER

    def issue(tok0, chunk, slot):
        base = tok0 + chunk * gch

        def body(r, carry):
            tok = tok_smem[base + r]
            pltpu.make_async_copy(
                h2_hbm.at[pl.ds(pl.multiple_of(tok * n_lc, n_lc), n_lc)],
                stage.at[slot, pl.ds(pl.multiple_of(r * STAGE_PITCH, SUBLANES), n_lc)],
                gsem.at[slot]).start(priority=1)
            return carry
        lax.fori_loop(0, gch, body, 0, unroll=8)

    def land(chunk, slot):
        pltpu.make_async_copy(h2_hbm.at[pl.ds(0, gch * n_lc)], stage.at[slot, pl.ds(0, gch * n_lc)],
                              gsem.at[slot]).wait()
        rows = rows_at(chunk * gch, gch)
        for lc in range(n_lc):
            xbuf[rows, lc * LANES:(lc + 1) * LANES] = (
                stage[slot, pl.ds(lc, gch, stride=STAGE_PITCH), :].astype(BF16))

    def n_chunks(n_gran_):
        return (n_gran_ * gran + gch - 1) // gch

    @pl.when(jnp.logical_and(s == 0, j == 0))
    def _first():
        pend[0] = 0
        pend[1] = 0

        nc = n_chunks(n_gran)
        issue(seg_tok0[s], 0, 0)

        def chunk_body(c, carry):
            slot = c % 2

            @pl.when(c + 1 < nc)
            def _():
                issue(seg_tok0[s], c + 1, 1 - slot)
            land(c, slot)
            return carry
        lax.fori_loop(0, nc, chunk_body, 0)

    def prefetch_next_rows(jb):
        has_next = s + 1 < pl.num_programs(0)
        s_next = jnp.minimum(s + 1, pl.num_programs(0) - 1)
        nc = jnp.where(has_next, n_chunks(seg_ngran[s_next]), 0)
        next_tok0 = seg_tok0[s_next]
        for q in range(2):
            @pl.when(jnp.logical_and(jb >= 1, 2 * (jb - 1) + q < nc))
            def _():
                land(2 * (jb - 1) + q, q)
        for q in range(2):
            @pl.when(jnp.logical_and(jb + 1 < MOE_NFB, 2 * jb + q < nc))
            def _():
                issue(next_tok0, 2 * jb + q, q)

    @pl.when(jnp.logical_and(j < MOE_NFA, live))
    def _up():
        wab[:, :fa] = wg_ref[0].astype(BF16)
        wab[:, fa:] = wl_ref[0].astype(BF16)
        bias = jnp.concatenate([bg_ref[0], bl_ref[0]], axis=1)

        def one(first, n_rows):
            rows = rows_at(first, n_rows)
            gu = jnp.dot(xbuf[rows, :], wab[...], preferred_element_type=F32) + bias
            glu = jnp.minimum(gu[:, :fa], SWIGLU_LIMIT)
            lin = jnp.clip(gu[:, fa:], -SWIGLU_LIMIT, SWIGLU_LIMIT)
            actbuf[j, rows, :] = (glu * jax.nn.sigmoid(SWIGLU_ALPHA * glu) * (lin + 1.0)).astype(BF16)
        _for_blocks(n_gran, one)

    def out_copy(slot, src_row, dst_row, col0):
        return pltpu.make_async_copy(ostage.at[slot, rows_at(src_row, gran)],
                                     y_hbm.at[rows_at(dst_row, gran), pl.ds(col0, fb)], osem.at[slot])

    def out_wait(slot):
        def body(i, carry):
            out_copy(slot, 0, 0, 0).wait()
            return carry
        lax.fori_loop(0, pend[slot], body, 0)
        pend[slot] = 0

    @pl.when(jnp.logical_and(j >= MOE_NFA, live))
    def _down():
        jb = j - MOE_NFA
        slot = jb % 2
        col0 = pl.multiple_of(jb * fb, fb)
        prefetch_next_rows(jb)
        wdb[...] = wd_ref[0].astype(BF16)
        out_wait(slot)

        def one(first, n_rows):
            rows = rows_at(first, n_rows)
            act = jnp.concatenate([actbuf[c, rows, :] for c in range(MOE_NFA)], axis=1)
            ostage[slot, rows, :] = jnp.dot(act, wdb[...], preferred_element_type=F32) + bd_ref[0]
            for h in range(n_rows // gran):
                out_copy(slot, first + h * gran, row0 + first + h * gran, col0).start()
        _for_blocks(n_gran, one)
        pend[slot] = n_gran

    @pl.when(jnp.logical_and(s == pl.num_programs(0) - 1, j == pl.num_programs(1) - 1))
    def _final():
        out_wait(0)
        out_wait(1)
        n_tail = y_hbm.shape[0] // gran - total_gran[0]

        @pl.when(n_tail > 0)
        def _():
            ostage[0, pl.ds(0, gran), :] = jnp.zeros((gran, fb), F32)

            def fill(t, carry):
                for cb in range(MOE_NFB):
                    out_copy(0, 0, (total_gran[0] + t) * gran, cb * fb).start()
                return carry
            lax.fori_loop(0, n_tail, fill, 0)
            pend[0] = n_tail * MOE_NFB
            out_wait(0)


def _moe(h2, n_live, seg_e, seg_ep, seg_en, seg_cn, seg_row0, seg_tok0, seg_ngran, total_gran, tok_sorted,
         w_gate_up, b_gate_up, w_down, b_down, cap):
    nfa, nfb = MOE_NFA, MOE_NFB
    assert MOE_MAXM <= 2 * (nfb - 1) * MOE_GATHER

    def up_block(s, j, e, en, cn):
        ahead = j >= nfa + nfb // 2
        eb = jnp.where(ahead, en[s], e[s])
        cb = jnp.where(j < nfa, j, jnp.where(ahead, cn[s], nfa - 1))
        return eb, cb

    def wg_map(s, j, e, ep, en, cn, r0, t0, ng, tg, tok):
        eb, cb = up_block(s, j, e, en, cn)
        return (eb, 0, cb)

    def wl_map(s, j, e, ep, en, cn, r0, t0, ng, tg, tok):
        eb, cb = up_block(s, j, e, en, cn)
        return (eb, 0, nfa + cb)

    def wd_map(s, j, e, ep, en, cn, r0, t0, ng, tg, tok):
        in_up = j < nfa
        eb = jnp.where(in_up, ep[s], e[s])
        jb = jnp.where(in_up, jnp.where(s > 0, nfb - 1, 0), j - nfa)
        return (eb, 0, jb)

    grid_spec = pltpu.PrefetchScalarGridSpec(
        num_scalar_prefetch=9,
        grid=(n_live, nfa + nfb),
        in_specs=[
            pl.BlockSpec(memory_space=pl.ANY),
            pl.BlockSpec((1, D_MODEL, MOE_FA), wg_map),
            pl.BlockSpec((1, D_MODEL, MOE_FA), wl_map),
            pl.BlockSpec((1, D_FF, MOE_FB), wd_map),
            pl.BlockSpec((1, 1, MOE_FA), wg_map),
            pl.BlockSpec((1, 1, MOE_FA), wl_map),
            pl.BlockSpec((1, 1, MOE_FB), wd_map),
        ],
        out_specs=pl.BlockSpec(memory_space=pl.ANY),
        scratch_shapes=[
            pltpu.VMEM((MOE_MAXM, D_MODEL), BF16),
            pltpu.VMEM((nfa, MOE_MAXM, MOE_FA), BF16),
            pltpu.VMEM((2, MOE_GATHER * STAGE_PITCH, LANES), F32),
            pltpu.VMEM((D_MODEL, 2 * MOE_FA), BF16),
            pltpu.VMEM((D_FF, MOE_FB), BF16),
            pltpu.VMEM((2, MOE_MAXM, MOE_FB), F32),
            pltpu.SMEM((2,), jnp.int32),
            pltpu.SemaphoreType.DMA((2,)),
            pltpu.SemaphoreType.DMA((2,)),
        ],
    )
    return pl.pallas_call(
        _moe_kernel,
        grid_spec=grid_spec,
        out_shape=jax.ShapeDtypeStruct((cap, D_MODEL), F32),
        compiler_params=_cparams(("arbitrary", "arbitrary")),
        name="moe",
    )(seg_e, seg_ep, seg_en, seg_cn, seg_row0, seg_tok0, seg_ngran, total_gran, tok_sorted,
      h2, w_gate_up, w_gate_up, w_down,
      b_gate_up.reshape(N_EXPERTS, 1, 2 * D_FF), b_gate_up.reshape(N_EXPERTS, 1, 2 * D_FF),
      b_down.reshape(N_EXPERTS, 1, D_MODEL))


def _combine_kernel(pos_ref, y_hbm, x1_ref, gate_ref, gfin_ref, op_ref, os_ref, buf, sem,
                    *, n_prompt_tiles):
    i = pl.program_id(0)
    n = pl.num_programs(0)
    tm = ROW_TM

    def row_copy(tile, slot, r, k):
        p = pos_ref[(tile * tm + r) * TOP_K + k]
        return pltpu.make_async_copy(y_hbm.at[pl.ds(p, 1)], buf.at[slot, k, pl.ds(r, 1)],
                                     sem.at[slot])

    def issue(tile, slot):
        def body(r, carry):
            for k in range(TOP_K):
                row_copy(tile, slot, r, k).start()
            return carry
        lax.fori_loop(0, tm, body, 0, unroll=4)

    def wait(slot):
        for k in range(TOP_K):
            pltpu.make_async_copy(y_hbm.at[pl.ds(0, tm)], buf.at[slot, k], sem.at[slot]).wait()

    slot = i % 2

    @pl.when(i == 0)
    def _():
        issue(0, 0)

    @pl.when(i + 1 < n)
    def _():
        issue(i + 1, 1 - slot)

    wait(slot)
    gate = gate_ref[...]
    x2 = x1_ref[...]
    for k in range(TOP_K):
        x2 = x2 + gate[:, k:k + 1] * buf[slot, k]
    ms = jnp.mean(x2 * x2, axis=-1, keepdims=True)
    y = x2 * lax.rsqrt(ms + EPS) * gfin_ref[...]

    @pl.when(i < n_prompt_tiles)
    def _():
        op_ref[...] = y

    @pl.when(i >= n_prompt_tiles)
    def _():
        os_ref[...] = y


def _combine(pos, y_sorted, x1, gate, g_final, n_prompt, n_sample):
    tm = ROW_TM
    n_tok = x1.shape[0]
    n_prompt_tiles = n_prompt // tm
    n_sample_tiles = n_sample // tm

    grid_spec = pltpu.PrefetchScalarGridSpec(
        num_scalar_prefetch=1,
        grid=(n_tok // tm,),
        in_specs=[
            pl.BlockSpec(memory_space=pl.ANY),
            pl.BlockSpec((tm, D_MODEL), lambda i, pos: (i, 0)),
            pl.BlockSpec((tm, TOP_K), lambda i, pos: (i, 0)),
            pl.BlockSpec((1, D_MODEL), lambda i, pos: (0, 0)),
        ],
        out_specs=[
            pl.BlockSpec((tm, D_MODEL), lambda i, pos: (jnp.minimum(i, n_prompt_tiles - 1), 0)),
            pl.BlockSpec((tm, D_MODEL),
                         lambda i, pos: (jnp.clip(i - n_prompt_tiles, 0, n_sample_tiles - 1), 0)),
        ],
        scratch_shapes=[
            pltpu.VMEM((2, TOP_K, tm, D_MODEL), F32),
            pltpu.SemaphoreType.DMA((2,)),
        ],
    )
    return pl.pallas_call(
        functools.partial(_combine_kernel, n_prompt_tiles=n_prompt_tiles),
        grid_spec=grid_spec,
        out_shape=[
            jax.ShapeDtypeStruct((n_prompt, D_MODEL), F32),
            jax.ShapeDtypeStruct((n_sample, D_MODEL), F32),
        ],
        compiler_params=_cparams(("arbitrary",)),
        name="combine",
    )(pos, y_sorted, x1, gate, g_final.reshape(1, D_MODEL))


def kernel(x_prompt, x_sample, cache_k, cache_v, state_pool, meta_tokens, g_mix, w_in, sinks,
           w_pool_mix, pool_scale, w_br_attn, w_br_pool, w_out, g_ffn, w_router, b_router,
           w_gate_up, b_gate_up, w_down, b_down, g_final):
    depth = w_in.shape[0]
    assert depth == 1, "single-layer step only"
    batch, seq, _ = x_prompt.shape
    dec_batch, dec_seq, _ = x_sample.shape
    n_prompt = batch * seq
    n_sample = dec_batch * dec_seq
    n_tok = n_prompt + n_sample
    assert seq % ATTN_TILE == 0 and n_prompt % ROW_TM == 0 and n_sample % ROW_TM == 0
    assert dec_seq == SUBLANES and dec_batch % SAMPLE_SEQS == 0

    assert n_prompt % PROJ_TM == 0 and n_sample + ATTN_TILE <= PROJ_TM
    xp = x_prompt.reshape(n_prompt, D_MODEL)
    xs = x_sample.reshape(n_sample, D_MODEL)
    pad = jnp.zeros((PROJ_TM - n_sample - N_META, D_MODEL), F32)
    x_tail = jnp.concatenate([xs, pad, meta_tokens.astype(F32)], axis=0)

    p_all = _inproj(_prenorm(xp, x_tail, g_mix[0]), w_in[0])

    ck = cache_k[0].reshape(dec_batch, WINDOW, KV_WIDTH)
    cv = cache_v[0].reshape(dec_batch, WINDOW, KV_WIDTH)
    state_pad = jnp.pad(state_pool[0], ((0, 0), (N_META - POOL_BUF, 0), (0, 0)))
    o_attn, z = _attn_prompt(p_all, sinks[0], batch, seq, n_tok)
    o_attn, z = _attn_sample(p_all, sinks[0], ck, cv, state_pad, o_attn, z, n_prompt, dec_batch,
                             dec_seq)

    wr = w_router[0]
    wr_hi = wr.astype(BF16)
    wr_lo = (wr - wr_hi.astype(F32)).astype(BF16)
    x1, h2, top_idx, gate, rank, counts = _finish(
        xp, xs, o_attn, z, p_all, w_pool_mix[0].astype(BF16), pool_scale[0],
        w_br_attn[0].astype(BF16), w_br_pool[0].astype(BF16), w_out[0].astype(BF16),
        g_ffn[0], jnp.concatenate([wr_hi, wr_lo], axis=1), b_router[0])

    n_assign = n_tok * TOP_K
    n_seg = N_EXPERTS + n_assign // MOE_MAXM
    cap = (n_assign // MOE_ROWS + N_EXPERTS) * MOE_ROWS
    (n_live, seg_e, seg_ep, seg_en, seg_cn, seg_row0, seg_tok0, seg_ngran, total_gran, tok_sorted,
     dest) = _route(top_idx, rank, counts, n_seg)
    y_sorted = _moe(h2, n_live, seg_e, seg_ep, seg_en, seg_cn, seg_row0, seg_tok0, seg_ngran, total_gran,
                    tok_sorted, w_gate_up[0], b_gate_up[0], w_down[0], b_down[0], cap)
    y_p, y_s = _combine(dest, y_sorted, x1, gate, g_final, n_prompt, n_sample)

    k0, v0, u0, u1 = ATTN_WIDTH, ATTN_WIDTH + KV_WIDTH, ATTN_WIDTH + 2 * KV_WIDTH, IN_COLS - 2 * D_MODEL

    def tail_rows(n, c0, c1):
        return jnp.stack([p_all[(b + 1) * seq - n:(b + 1) * seq, c0:c1] for b in range(batch)])

    new_k_p = tail_rows(WINDOW, k0, v0).reshape(1, batch, WINDOW, N_KV_HEADS, HEAD_DIM)
    new_v_p = tail_rows(WINDOW, v0, u0).reshape(1, batch, WINDOW, N_KV_HEADS, HEAD_DIM)
    new_u_p = tail_rows(POOL_BUF, u0, u1)[None]
    ps = p_all[n_prompt:n_tok, k0:u1].reshape(dec_batch, dec_seq, u1 - k0)
    new_k_s = jnp.concatenate([ck[:, dec_seq:], ps[:, :, :KV_WIDTH]], axis=1).reshape(
        1, dec_batch, WINDOW, N_KV_HEADS, HEAD_DIM)
    new_v_s = jnp.concatenate([cv[:, dec_seq:], ps[:, :, KV_WIDTH:2 * KV_WIDTH]], axis=1).reshape(
        1, dec_batch, WINDOW, N_KV_HEADS, HEAD_DIM)
    new_u_s = jnp.concatenate([state_pool[0][:, dec_seq:], ps[:, :, 2 * KV_WIDTH:]], axis=1)[None]

    return (y_p.reshape(batch, seq, D_MODEL), y_s.reshape(dec_batch, dec_seq, D_MODEL),
            new_k_p, new_v_p, new_u_p, new_k_s, new_v_s, new_u_s)
```

```python
import functools

import jax
import jax.numpy as jnp
from jax import lax
from jax.experimental import pallas as pl
from jax.experimental.pallas import tpu as pltpu

F32 = jnp.float32
BF16 = jnp.bfloat16

D_MODEL = 2048
N_META = 16
N_HEADS = 32
N_KV_HEADS = 8
HEAD_DIM = 64
GROUP = N_HEADS // N_KV_HEADS
WINDOW = 128
ATTN_WIDTH = N_HEADS * HEAD_DIM
KV_WIDTH = N_KV_HEADS * HEAD_DIM
POOL_WIDTH = D_MODEL // 2
POOL_WINDOWS = (2, 4, 8, 16)
POOL_GROUP_DIM = POOL_WIDTH // len(POOL_WINDOWS)
POOL_BUF = max(POOL_WINDOWS) - 1
N_EXPERTS = 32
TOP_K = 4
D_FF = D_MODEL
SWIGLU_LIMIT = 7.0
SWIGLU_ALPHA = 1.702
EPS = 1e-5
IN_COLS = ATTN_WIDTH + 2 * KV_WIDTH + POOL_WIDTH + 2 * D_MODEL

LANES = 128
SUBLANES = 8
VMEM_LIMIT_BYTES = 58 * 1024 * 1024

ATTN_TILE = WINDOW
PROJ_TM = 512
PROJ_TN = 1024
ROW_TM = 256
SAMPLE_SEQS = 4
MOE_ROWS = 128
MOE_SUB = 2 * MOE_ROWS
MOE_MAXM = 12 * MOE_ROWS
MOE_FA = 512
MOE_FB = 512
MOE_GATHER = 2 * MOE_ROWS
MOE_KSLABS = 4
MOE_NFA = D_FF // MOE_FA
MOE_NFB = D_MODEL // MOE_FB
STAGE_PITCH = 3 * SUBLANES
NEG_BIG = -1e30

_SLOPES = tuple(float(2.0 ** (-8.0 * (i + 1) / N_HEADS)) for i in range(N_HEADS))


def _cparams(sem, vmem=VMEM_LIMIT_BYTES):
    return pltpu.CompilerParams(dimension_semantics=sem, vmem_limit_bytes=vmem)


def _rmsnorm(x, g):
    ms = jnp.mean(x * x, axis=-1, keepdims=True)
    return x * lax.rsqrt(ms + EPS) * g


def _prenorm_kernel(xp_ref, xt_ref, g_ref, h_ref, *, n_main):
    x = jnp.where(pl.program_id(0) < n_main, xp_ref[...], xt_ref[...])
    h_ref[...] = _rmsnorm(x, g_ref[...]).astype(h_ref.dtype)


def _prenorm(x_main, x_tail, g_mix):
    n_main = x_main.shape[0] // PROJ_TM
    n_rows = x_main.shape[0] + x_tail.shape[0]
    return pl.pallas_call(
        functools.partial(_prenorm_kernel, n_main=n_main),
        grid=(n_main + 1,),
        in_specs=[
            pl.BlockSpec((PROJ_TM, D_MODEL), lambda m: (jnp.minimum(m, n_main - 1), 0)),
            pl.BlockSpec((PROJ_TM, D_MODEL), lambda m: (0, 0)),
            pl.BlockSpec((1, D_MODEL), lambda m: (0, 0)),
        ],
        out_specs=pl.BlockSpec((PROJ_TM, D_MODEL), lambda m: (m, 0)),
        out_shape=jax.ShapeDtypeStruct((n_rows, D_MODEL), BF16),
        compiler_params=_cparams(("arbitrary",)),
        name="prenorm",
    )(x_main, x_tail, g_mix.reshape(1, D_MODEL))


def _inproj_kernel(h_ref, w_ref, o_ref, wbf_ref):
    @pl.when(pl.program_id(1) == 0)
    def _():
        wbf_ref[...] = w_ref[...].astype(BF16)

    o_ref[...] = jnp.dot(h_ref[...], wbf_ref[...], preferred_element_type=F32)


def _inproj(h, w_in):
    n_rows = h.shape[0]
    grid = (IN_COLS // PROJ_TN, n_rows // PROJ_TM)
    return pl.pallas_call(
        _inproj_kernel,
        grid=grid,
        in_specs=[
            pl.BlockSpec((PROJ_TM, D_MODEL), lambda n, m: (m, 0)),
            pl.BlockSpec((D_MODEL, PROJ_TN), lambda n, m: (0, n)),
        ],
        out_specs=pl.BlockSpec((PROJ_TM, PROJ_TN), lambda n, m: (m, n)),
        out_shape=jax.ShapeDtypeStruct((n_rows, IN_COLS), F32),
        scratch_shapes=[pltpu.VMEM((D_MODEL, PROJ_TN), BF16)],
        compiler_params=_cparams(("arbitrary", "arbitrary")),
        name="inproj",
    )(h, w_in)


def _attn_bias(n_q, n_keys, first_key):
    r = jnp.arange(n_q, dtype=jnp.int32)[:, None]
    c = jnp.arange(n_keys, dtype=jnp.int32)[None, :]
    dist = r + WINDOW - c
    valid = (dist >= 0) & (dist <= WINDOW) & (c >= first_key)
    slopes = jnp.asarray(_SLOPES, F32).reshape(N_KV_HEADS, GROUP, 1, 1)
    bias = jnp.where(valid[None, None], -slopes * dist.astype(F32)[None, None], NEG_BIG)
    return bias.reshape(N_KV_HEADS, GROUP * n_q, n_keys)


def _group_sinks(sink_ref, kvh, n_q):
    g_row = lax.broadcasted_iota(jnp.int32, (GROUP * n_q, 1), 0) // n_q
    sink = jnp.zeros((GROUP * n_q, 1), F32)
    for g in range(GROUP):
        sink = jnp.where(g_row == g, sink_ref[kvh * GROUP + g], sink)
    return sink


def _group_attention(q, k, v, bias, sink_ref, kvh, n_q):
    kh = k[:, kvh * HEAD_DIM:(kvh + 1) * HEAD_DIM]
    vh = v[:, kvh * HEAD_DIM:(kvh + 1) * HEAD_DIM]
    heads = [kvh * GROUP + g for g in range(GROUP)]
    qg = jnp.concatenate([q[:, hd * HEAD_DIM:(hd + 1) * HEAD_DIM] for hd in heads], axis=0)
    s = lax.dot_general(qg, kh, (((1,), (1,)), ((), ())), preferred_element_type=F32) + bias
    sink = _group_sinks(sink_ref, kvh, n_q)
    m = jnp.maximum(jnp.max(s, axis=-1, keepdims=True), sink)
    p = jnp.exp(s - m)
    den = jnp.sum(p, axis=-1, keepdims=True) + jnp.exp(sink - m)
    o = jnp.dot(p.astype(BF16), vh, preferred_element_type=F32)
    return o / den


def _pool_features(ext, n_halo):
    outs = []
    for g, w in enumerate(POOL_WINDOWS):
        e = ext[:, g * POOL_GROUP_DIM:(g + 1) * POOL_GROUP_DIM]
        s = e
        shift = 1
        while shift < w:
            s = s + pltpu.roll(s, shift, axis=0)
            shift *= 2
        outs.append(s[n_halo:] * (1.0 / w) - e[n_halo:])
    return jnp.concatenate(outs, axis=1)


def _attn_prompt_kernel(sink_ref, bias_ref, q_ref, ko_ref, kp_ref, vo_ref, vp_ref, uo_ref, up_ref,
                        o_ref, z_ref, s_scr, p_scr, *, n_prompt_tiles):
    t = ATTN_TILE
    step = pl.program_id(0)

    @pl.when(step < n_prompt_tiles)
    def _():
        q = (q_ref[...] * (HEAD_DIM ** -0.5)).astype(BF16)
        k = jnp.concatenate([kp_ref[...], ko_ref[...]], axis=0).astype(BF16)
        v = jnp.concatenate([vp_ref[...], vo_ref[...]], axis=0).astype(BF16)
        for kvh in range(N_KV_HEADS):
            kh = k[:, kvh * HEAD_DIM:(kvh + 1) * HEAD_DIM]
            qg = jnp.concatenate([q[:, (kvh * GROUP + g) * HEAD_DIM:(kvh * GROUP + g + 1) * HEAD_DIM]
                                  for g in range(GROUP)], axis=0)
            s_scr[kvh] = lax.dot_general(qg, kh, (((1,), (1,)), ((), ())),
                                         preferred_element_type=F32) + bias_ref[0, kvh]
        for kvh in range(N_KV_HEADS):
            s = s_scr[kvh]
            sink = _group_sinks(sink_ref, kvh, t)
            m = jnp.maximum(jnp.max(s, axis=-1, keepdims=True), sink)
            p = jnp.exp(s - m)
            den = jnp.sum(p, axis=-1, keepdims=True) + jnp.exp(sink - m)
            p_scr[kvh] = (p / den).astype(BF16)
        for kvh in range(N_KV_HEADS):
            vh = v[:, kvh * HEAD_DIM:(kvh + 1) * HEAD_DIM]
            o = jnp.dot(p_scr[kvh], vh, preferred_element_type=F32)
            for g in range(GROUP):
                hd = kvh * GROUP + g
                o_ref[:, hd * HEAD_DIM:(hd + 1) * HEAD_DIM] = o[g * t:(g + 1) * t].astype(o_ref.dtype)

        ext = jnp.concatenate([up_ref[...], uo_ref[...]], axis=0)
        z_ref[...] = _pool_features(ext, N_META).astype(z_ref.dtype)

    @pl.when(step >= n_prompt_tiles)
    def _():
        o_ref[...] = jnp.zeros_like(o_ref)
        z_ref[...] = jnp.zeros_like(z_ref)


def _attn_prompt(p_all, sinks, batch, seq, n_out_rows):
    t = ATTN_TILE
    tiles = seq // t
    n_prompt_tiles = batch * tiles
    assert n_out_rows % t == 0
    meta_blk = p_all.shape[0] // t - 1
    q_w, kv_w, u_w = ATTN_WIDTH, KV_WIDTH, POOL_WIDTH
    kcol, vcol, ucol = ATTN_WIDTH // kv_w, ATTN_WIDTH // kv_w + 1, (ATTN_WIDTH + 2 * kv_w) // u_w
    halo_per_tile = t // N_META
    bias = jnp.stack([_attn_bias(t, 2 * t, t - N_META), _attn_bias(t, 2 * t, 0)])

    def prev(s):
        return jnp.where(s % tiles > 0, s - 1, meta_blk)

    return pl.pallas_call(
        functools.partial(_attn_prompt_kernel, n_prompt_tiles=n_prompt_tiles),
        grid=(n_out_rows // t,),
        in_specs=[
            pl.BlockSpec(memory_space=pltpu.SMEM),
            pl.BlockSpec((1, N_KV_HEADS, GROUP * t, 2 * t), lambda s: (jnp.minimum(s % tiles, 1), 0, 0, 0)),
            pl.BlockSpec((t, q_w), lambda s: (s, 0)),
            pl.BlockSpec((t, kv_w), lambda s: (s, kcol)),
            pl.BlockSpec((t, kv_w), lambda s: (prev(s), kcol)),
            pl.BlockSpec((t, kv_w), lambda s: (s, vcol)),
            pl.BlockSpec((t, kv_w), lambda s: (prev(s), vcol)),
            pl.BlockSpec((t, u_w), lambda s: (s, ucol)),
            pl.BlockSpec((N_META, u_w), lambda s: (prev(s) * halo_per_tile + halo_per_tile - 1, ucol)),
        ],
        out_specs=[
            pl.BlockSpec((t, q_w), lambda s: (s, 0)),
            pl.BlockSpec((t, u_w), lambda s: (s, 0)),
        ],
        out_shape=[
            jax.ShapeDtypeStruct((n_out_rows, q_w), BF16),
            jax.ShapeDtypeStruct((n_out_rows, u_w), BF16),
        ],
        scratch_shapes=[
            pltpu.VMEM((N_KV_HEADS, GROUP * t, 2 * t), F32),
            pltpu.VMEM((N_KV_HEADS, GROUP * t, 2 * t), BF16),
        ],
        compiler_params=_cparams(("arbitrary",)),
        name="attn_prompt",
    )(sinks, bias, p_all, p_all, p_all, p_all, p_all, p_all, p_all)


def _attn_sample_kernel(sink_ref, bias_ref, q_ref, kn_ref, vn_ref, un_ref, ck_ref, cv_ref, sp_ref,
                        o_prev_ref, z_prev_ref, o_ref, z_ref):
    del o_prev_ref, z_prev_ref
    n_seqs = ck_ref.shape[0]
    n_new = q_ref.shape[0] // n_seqs
    q_all = (q_ref[...] * (HEAD_DIM ** -0.5)).astype(BF16)
    for si in range(n_seqs):
        rows = slice(si * n_new, (si + 1) * n_new)
        q = q_all[rows]
        k = jnp.concatenate([ck_ref[si], kn_ref[rows, :]], axis=0).astype(BF16)
        v = jnp.concatenate([cv_ref[si], vn_ref[rows, :]], axis=0).astype(BF16)
        for kvh in range(N_KV_HEADS):
            o = _group_attention(q, k, v, bias_ref[kvh], sink_ref, kvh, n_new)
            for g in range(GROUP):
                hd = kvh * GROUP + g
                o_ref[rows, hd * HEAD_DIM:(hd + 1) * HEAD_DIM] = (
                    o[g * n_new:(g + 1) * n_new].astype(o_ref.dtype))
        ext = jnp.concatenate([sp_ref[si], un_ref[rows, :]], axis=0)
        z_ref[rows, :] = _pool_features(ext, sp_ref.shape[1]).astype(z_ref.dtype)


def _attn_sample(p_all, sinks, cache_k, cache_v, state_pad, o_attn, z, row0, dec_batch, dec_seq):
    q_w, kv_w, u_w = ATTN_WIDTH, KV_WIDTH, POOL_WIDTH
    kcol, vcol, ucol = ATTN_WIDTH // kv_w, ATTN_WIDTH // kv_w + 1, (ATTN_WIDTH + 2 * kv_w) // u_w
    ns = SAMPLE_SEQS
    rows = ns * dec_seq
    blk0 = row0 // rows
    n_halo = state_pad.shape[1]
    n_keys = WINDOW + dec_seq
    bias = _attn_bias(dec_seq, n_keys, 0)
    return pl.pallas_call(
        _attn_sample_kernel,
        grid=(dec_batch // ns,),
        in_specs=[
            pl.BlockSpec(memory_space=pltpu.SMEM),
            pl.BlockSpec((N_KV_HEADS, GROUP * dec_seq, n_keys), lambda b: (0, 0, 0)),
            pl.BlockSpec((rows, q_w), lambda b: (blk0 + b, 0)),
            pl.BlockSpec((rows, kv_w), lambda b: (blk0 + b, kcol)),
            pl.BlockSpec((rows, kv_w), lambda b: (blk0 + b, vcol)),
            pl.BlockSpec((rows, u_w), lambda b: (blk0 + b, ucol)),
            pl.BlockSpec((ns, WINDOW, kv_w), lambda b: (b, 0, 0)),
            pl.BlockSpec((ns, WINDOW, kv_w), lambda b: (b, 0, 0)),
            pl.BlockSpec((ns, n_halo, u_w), lambda b: (b, 0, 0)),
            pl.BlockSpec(memory_space=pl.ANY),
            pl.BlockSpec(memory_space=pl.ANY),
        ],
        out_specs=[
            pl.BlockSpec((rows, q_w), lambda b: (blk0 + b, 0)),
            pl.BlockSpec((rows, u_w), lambda b: (blk0 + b, 0)),
        ],
        out_shape=[
            jax.ShapeDtypeStruct(o_attn.shape, o_attn.dtype),
            jax.ShapeDtypeStruct(z.shape, z.dtype),
        ],
        input_output_aliases={9: 0, 10: 1},
        compiler_params=_cparams(("arbitrary",)),
        name="attn_sample",
    )(sinks, bias, p_all, p_all, p_all, p_all, cache_k, cache_v, state_pad, o_attn, z)


def _finish_kernel(xp_ref, xs_ref, oa_ref, z_ref, ag_ref, pg_ref, wpm_ref, ps_ref, wba_ref, wbp_ref,
                   wo_ref, gf_ref, wr_ref, br_ref,
                   x1_ref, h2_ref, idx_ref, gate_ref, rank_ref, cnt_ref, carry_ref,
                   *, n_prompt_tiles):
    tm = xp_ref.shape[0]
    i = pl.program_id(0)

    @pl.when(i == 0)
    def _():
        carry_ref[...] = jnp.zeros_like(carry_ref)

    x_in = jnp.where(i < n_prompt_tiles, xp_ref[...], xs_ref[...])

    z = z_ref[...]
    zp = jnp.concatenate(
        [jnp.dot(z[:, g * POOL_GROUP_DIM:(g + 1) * POOL_GROUP_DIM], wpm_ref[g],
                 preferred_element_type=F32) for g in range(len(POOL_WINDOWS))], axis=1)
    zp = (zp * ps_ref[...]).astype(BF16)
    ya = jnp.dot(oa_ref[...], wba_ref[...], preferred_element_type=F32)
    yp = jnp.dot(zp, wbp_ref[...], preferred_element_type=F32)
    merged = jax.nn.sigmoid(ag_ref[...]) * ya + jax.nn.sigmoid(pg_ref[...]) * yp
    x1 = x_in + jnp.dot(merged.astype(BF16), wo_ref[...], preferred_element_type=F32)
    x1_ref[...] = x1
    h2 = _rmsnorm(x1, gf_ref[...])
    n_lc = D_MODEL // LANES
    for c in range(n_lc):
        h2_ref[pl.ds(c, tm, stride=n_lc), :] = h2[:, c * LANES:(c + 1) * LANES]

    h_hi = h2.astype(BF16)
    h_lo = (h2 - h_hi.astype(F32)).astype(BF16)
    t = jnp.dot(h_hi, wr_ref[...], preferred_element_type=F32)
    logits = (t[:, :N_EXPERTS] + t[:, N_EXPERTS:]
              + jnp.dot(h_lo, wr_ref[:, :N_EXPERTS], preferred_element_type=F32) + br_ref[...])

    col = lax.broadcasted_iota(jnp.int32, logits.shape, 1).astype(F32)
    vals, idxs = [], []
    for _ in range(TOP_K):
        m = jnp.max(logits, axis=-1, keepdims=True)
        idx = jnp.min(jnp.where(logits == m, col, float(N_EXPERTS)), axis=-1, keepdims=True)
        vals.append(m)
        idxs.append(idx)
        logits = jnp.where(col == idx, -jnp.inf, logits)
    exps = [jnp.exp(v - vals[0]) for v in vals]
    den = exps[0] + exps[1] + exps[2] + exps[3]

    member = jnp.zeros(logits.shape, F32)
    for k in range(TOP_K):
        member = member + (col == idxs[k]).astype(F32)
    ri = lax.broadcasted_iota(jnp.int32, (tm, tm), 0)
    ci = lax.broadcasted_iota(jnp.int32, (tm, tm), 1)
    earlier = (ri > ci).astype(BF16)
    before = jnp.dot(earlier, member.astype(BF16), preferred_element_type=F32) + carry_ref[...]
    for k in range(TOP_K):
        idx_ref[:, k:k + 1] = idxs[k].astype(jnp.int32)
        gate_ref[:, k:k + 1] = exps[k] / den
        rank_ref[:, k:k + 1] = jnp.sum(jnp.where(col == idxs[k], before, 0.0), axis=-1,
                                       keepdims=True).astype(jnp.int32)
    carry_ref[...] += jnp.sum(member, axis=0, keepdims=True)
    cnt_ref[...] = carry_ref[...]


def _finish(x_prompt, x_sample, o_attn, z, p_all, wpm, pool_scale, wba, wbp, wo, g_ffn, w_router2,
            b_router):
    n_tok = o_attn.shape[0]
    tm = ROW_TM
    n_prompt_tiles = x_prompt.shape[0] // tm
    n_sample_tiles = x_sample.shape[0] // tm
    acol, pcol = (IN_COLS - 2 * D_MODEL) // D_MODEL, (IN_COLS - D_MODEL) // D_MODEL
    const = pl.Buffered(1)

    def whole(shape):
        nd = len(shape)
        return pl.BlockSpec(shape, lambda i: (0,) * nd, pipeline_mode=const)

    return pl.pallas_call(
        functools.partial(_finish_kernel, n_prompt_tiles=n_prompt_tiles),
        grid=(n_tok // tm,),
        in_specs=[
            pl.BlockSpec((tm, D_MODEL), lambda i: (jnp.minimum(i, n_prompt_tiles - 1), 0)),
            pl.BlockSpec((tm, D_MODEL),
                         lambda i: (jnp.clip(i - n_prompt_tiles, 0, n_sample_tiles - 1), 0)),
            pl.BlockSpec((tm, ATTN_WIDTH), lambda i: (i, 0)),
            pl.BlockSpec((tm, POOL_WIDTH), lambda i: (i, 0)),
            pl.BlockSpec((tm, D_MODEL), lambda i: (i, acol)),
            pl.BlockSpec((tm, D_MODEL), lambda i: (i, pcol)),
            whole(wpm.shape), whole((1, POOL_WIDTH)), whole(wba.shape), whole(wbp.shape),
            whole(wo.shape), whole((1, D_MODEL)), whole(w_router2.shape), whole((1, N_EXPERTS)),
        ],
        out_specs=[
            pl.BlockSpec((tm, D_MODEL), lambda i: (i, 0)),
            pl.BlockSpec((tm * (D_MODEL // LANES), LANES), lambda i: (i, 0)),
            pl.BlockSpec((tm, TOP_K), lambda i: (i, 0)),
            pl.BlockSpec((tm, TOP_K), lambda i: (i, 0)),
            pl.BlockSpec((tm, TOP_K), lambda i: (i, 0)),
            pl.BlockSpec((1, N_EXPERTS), lambda i: (0, 0)),
        ],
        out_shape=[
            jax.ShapeDtypeStruct((n_tok, D_MODEL), F32),
            jax.ShapeDtypeStruct((n_tok * (D_MODEL // LANES), LANES), F32),
            jax.ShapeDtypeStruct((n_tok, TOP_K), jnp.int32),
            jax.ShapeDtypeStruct((n_tok, TOP_K), F32),
            jax.ShapeDtypeStruct((n_tok, TOP_K), jnp.int32),
            jax.ShapeDtypeStruct((1, N_EXPERTS), F32),
        ],
        scratch_shapes=[pltpu.VMEM((1, N_EXPERTS), F32)],
        compiler_params=_cparams(("arbitrary",)),
        name="finish",
    )(x_prompt, x_sample, o_attn, z, p_all, p_all, wpm, pool_scale.reshape(1, POOL_WIDTH), wba, wbp,
      wo, g_ffn.reshape(1, D_MODEL), w_router2, b_router.reshape(1, N_EXPERTS))


def _route(top_idx, rank, counts_f, n_seg):
    counts = counts_f.reshape(N_EXPERTS).astype(jnp.int32)
    padded = (counts + MOE_ROWS - 1) // MOE_ROWS * MOE_ROWS
    pend = jnp.cumsum(padded)
    pstart = pend - padded
    experts = jnp.arange(N_EXPERTS, dtype=jnp.int32)
    start_of = jnp.sum(jnp.where(top_idx[..., None] == experts, pstart, 0), axis=-1)
    dest = (start_of + rank).reshape(-1)
    _, tok_order = lax.sort_key_val(dest, jnp.arange(dest.shape[0], dtype=jnp.int32) // TOP_K)
    tok_sorted = jnp.concatenate([tok_order, jnp.zeros((MOE_GATHER,), jnp.int32)])
    cstart = jnp.cumsum(counts) - counts

    nseg_e = (counts + MOE_MAXM - 1) // MOE_MAXM
    seg_end = jnp.cumsum(nseg_e)
    seg_base = seg_end - nseg_e
    s_ids = jnp.arange(n_seg, dtype=jnp.int32)
    last = jnp.maximum(seg_end[-1] - 1, 0)
    s_eff = jnp.minimum(s_ids, last)
    e_of_s = jnp.minimum(jnp.sum(s_eff[:, None] >= seg_end[None, :], axis=1), N_EXPERTS - 1)
    e_prev = jnp.concatenate([e_of_s[:1], e_of_s[:-1]])
    is_last = s_ids >= last
    e_next = jnp.where(is_last, e_of_s, jnp.concatenate([e_of_s[1:], e_of_s[-1:]]))
    c_next = jnp.where(is_last, MOE_NFA - 1, 0)
    k_in = s_eff - seg_base[e_of_s]
    nrows = jnp.clip(counts[e_of_s] - k_in * MOE_MAXM, 0, MOE_MAXM)
    n_gran = (nrows + MOE_ROWS - 1) // MOE_ROWS
    row0 = pstart[e_of_s] + k_in * MOE_MAXM
    tok0 = cstart[e_of_s] + k_in * MOE_MAXM
    total_gran = (pend[-1] // MOE_ROWS).reshape(1)
    i32 = jnp.int32
    n_live = (last + 1).astype(i32)
    return (n_live, e_of_s.astype(i32), e_prev.astype(i32), e_next.astype(i32), c_next.astype(i32),
            row0.astype(i32), tok0.astype(i32), n_gran.astype(i32), total_gran.astype(i32), tok_sorted,
            dest.astype(i32))


def _for_blocks(base, n_gran, fn):
    n_full = n_gran // 2

    def pair(p, carry):
        r = base + p * (2 * MOE_SUB)
        fn(r, MOE_SUB)
        fn(r + MOE_SUB, MOE_SUB)
        return carry
    lax.fori_loop(0, n_full // 2, pair, 0)

    @pl.when(n_full % 2 == 1)
    def _():
        fn(base + (n_full - 1) * MOE_SUB, MOE_SUB)

    @pl.when(n_gran % 2 == 1)
    def _():
        fn(base + n_full * MOE_SUB, MOE_ROWS)


def _moe_kernel(seg_e, seg_ep, seg_en, seg_cn, seg_row0, seg_tok0, seg_ngran, total_gran, tok_smem,
                h2_hbm, wg_ref, wl_ref, wd_ref, bg_ref, bl_ref, bd_ref,
                y_hbm,
                xbuf, actbuf, stage, wab, wdb, ostage, pend, gsem, osem):
    del seg_e, seg_ep, seg_en, seg_cn
    s = pl.program_id(0)
    j = pl.program_id(1)
    n_gran = seg_ngran[s]
    row0 = seg_row0[s]
    live = n_gran > 0
    gran = MOE_ROWS
    fa, fb = MOE_FA, MOE_FB
    n_lc = D_MODEL // LANES
    kslab = D_MODEL // MOE_KSLABS

    def rows_at(first, n_rows):
        return pl.ds(pl.multiple_of(first, MOE_ROWS), n_rows)

    gch = MOE_GATHER

    def issue(tok0, chunk, slot):
        base = tok0 + chunk * gch

        def body(r, carry):
            tok = tok_smem[base + r]
            pltpu.make_async_copy(
                h2_hbm.at[pl.ds(pl.multiple_of(tok * n_lc, n_lc), n_lc)],
                stage.at[slot, pl.ds(pl.multiple_of(r * STAGE_PITCH, SUBLANES), n_lc)],
                gsem.at[slot]).start(priority=1)
            return carry
        lax.fori_loop(0, gch, body, 0, unroll=8)

    def land(chunk, slot):
        pltpu.make_async_copy(h2_hbm.at[pl.ds(0, gch * n_lc)], stage.at[slot, pl.ds(0, gch * n_lc)],
                              gsem.at[slot]).wait()
        rows = rows_at(chunk * gch, gch)
        for lc in range(n_lc):
            xbuf[rows, lc * LANES:(lc + 1) * LANES] = (
                stage[slot, pl.ds(lc, gch, stride=STAGE_PITCH), :].astype(BF16))

    def n_chunks(n_gran_):
        return (n_gran_ * gran + gch - 1) // gch

    @pl.when(jnp.logical_and(s == 0, j == 0))
    def _first():
        pend[0] = 0
        pend[1] = 0

        nc = n_chunks(n_gran)
        issue(seg_tok0[s], 0, 0)

        def chunk_body(c, carry):
            slot = c % 2

            @pl.when(c + 1 < nc)
            def _():
                issue(seg_tok0[s], c + 1, 1 - slot)
            land(c, slot)
            return carry
        lax.fori_loop(0, nc, chunk_body, 0)

    def prefetch_next_rows(jb):
        has_next = s + 1 < pl.num_programs(0)
        s_next = jnp.minimum(s + 1, pl.num_programs(0) - 1)
        nc = jnp.where(has_next, n_chunks(seg_ngran[s_next]), 0)
        next_tok0 = seg_tok0[s_next]
        for q in range(2):
            @pl.when(jnp.logical_and(jb >= 1, 2 * (jb - 1) + q < nc))
            def _():
                land(2 * (jb - 1) + q, q)
        for q in range(2):
            @pl.when(jnp.logical_and(jb + 1 < MOE_NFB, 2 * jb + q < nc))
            def _():
                issue(next_tok0, 2 * jb + q, q)

    @pl.when(jnp.logical_and(j < MOE_NFA, live))
    def _up():
        bias = jnp.concatenate([bg_ref[0], bl_ref[0]], axis=1)

        def activate(rows, gu):
            glu = jnp.minimum(gu[:, :fa], SWIGLU_LIMIT)
            lin = jnp.clip(gu[:, fa:], -SWIGLU_LIMIT, SWIGLU_LIMIT)
            actbuf[j, rows, :] = (glu * jax.nn.sigmoid(SWIGLU_ALPHA * glu) * (lin + 1.0)).astype(BF16)

        rows0 = pl.ds(0, gran)
        gu = bias
        for c in range(MOE_KSLABS):
            ks = slice(c * kslab, (c + 1) * kslab)
            wab[ks, :fa] = wg_ref[0, ks, :].astype(BF16)
            wab[ks, fa:] = wl_ref[0, ks, :].astype(BF16)
            gu = gu + jnp.dot(xbuf[rows0, ks], wab[ks, :], preferred_element_type=F32)
        activate(rows0, gu)

        def one(first, n_rows):
            rows = rows_at(first, n_rows)
            activate(rows, jnp.dot(xbuf[rows, :], wab[...], preferred_element_type=F32) + bias)
        _for_blocks(gran, n_gran - 1, one)

    def out_copy(slot, src_row, dst_row, col0):
        return pltpu.make_async_copy(ostage.at[slot, rows_at(src_row, gran)],
                                     y_hbm.at[rows_at(dst_row, gran), pl.ds(col0, fb)], osem.at[slot])

    def out_wait(slot):
        def body(i, carry):
            out_copy(slot, 0, 0, 0).wait()
            return carry
        lax.fori_loop(0, pend[slot], body, 0)
        pend[slot] = 0

    @pl.when(jnp.logical_and(j >= MOE_NFA, live))
    def _down():
        jb = j - MOE_NFA
        slot = jb % 2
        col0 = pl.multiple_of(jb * fb, fb)
        prefetch_next_rows(jb)
        out_wait(slot)

        rows0 = pl.ds(0, gran)
        yd = bd_ref[0]
        for c in range(MOE_NFA):
            ks = slice(c * fa, (c + 1) * fa)
            wdb[ks, :] = wd_ref[0, ks, :].astype(BF16)
            yd = yd + jnp.dot(actbuf[c, rows0, :], wdb[ks, :], preferred_element_type=F32)
        ostage[slot, rows0, :] = yd
        out_copy(slot, 0, row0, col0).start()

        def one(first, n_rows):
            rows = rows_at(first, n_rows)
            act = jnp.concatenate([actbuf[c, rows, :] for c in range(MOE_NFA)], axis=1)
            ostage[slot, rows, :] = jnp.dot(act, wdb[...], preferred_element_type=F32) + bd_ref[0]
            for h in range(n_rows // gran):
                out_copy(slot, first + h * gran, row0 + first + h * gran, col0).start()
        _for_blocks(gran, n_gran - 1, one)
        pend[slot] = n_gran

    @pl.when(jnp.logical_and(s == pl.num_programs(0) - 1, j == pl.num_programs(1) - 1))
    def _final():
        out_wait(0)
        out_wait(1)
        n_tail = y_hbm.shape[0] // gran - total_gran[0]

        @pl.when(n_tail > 0)
        def _():
            ostage[0, pl.ds(0, gran), :] = jnp.zeros((gran, fb), F32)

            def fill(t, carry):
                for cb in range(MOE_NFB):
                    out_copy(0, 0, (total_gran[0] + t) * gran, cb * fb).start()
                return carry
            lax.fori_loop(0, n_tail, fill, 0)
            pend[0] = n_tail * MOE_NFB
            out_wait(0)


def _moe(h2, n_live, seg_e, seg_ep, seg_en, seg_cn, seg_row0, seg_tok0, seg_ngran, total_gran, tok_sorted,
         w_gate_up, b_gate_up, w_down, b_down, cap):
    nfa, nfb = MOE_NFA, MOE_NFB
    assert MOE_MAXM <= 2 * (nfb - 1) * MOE_GATHER

    def up_block(s, j, e, en, cn):
        ahead = j >= nfa + nfb // 2
        eb = jnp.where(ahead, en[s], e[s])
        cb = jnp.where(j < nfa, j, jnp.where(ahead, cn[s], nfa - 1))
        return eb, cb

    def wg_map(s, j, e, ep, en, cn, r0, t0, ng, tg, tok):
        eb, cb = up_block(s, j, e, en, cn)
        return (eb, 0, cb)

    def wl_map(s, j, e, ep, en, cn, r0, t0, ng, tg, tok):
        eb, cb = up_block(s, j, e, en, cn)
        return (eb, 0, nfa + cb)

    def wd_map(s, j, e, ep, en, cn, r0, t0, ng, tg, tok):
        in_up = j < nfa
        eb = jnp.where(in_up, ep[s], e[s])
        jb = jnp.where(in_up, jnp.where(s > 0, nfb - 1, 0), j - nfa)
        return (eb, 0, jb)

    grid_spec = pltpu.PrefetchScalarGridSpec(
        num_scalar_prefetch=9,
        grid=(n_live, nfa + nfb),
        in_specs=[
            pl.BlockSpec(memory_space=pl.ANY),
            pl.BlockSpec((1, D_MODEL, MOE_FA), wg_map),
            pl.BlockSpec((1, D_MODEL, MOE_FA), wl_map),
            pl.BlockSpec((1, D_FF, MOE_FB), wd_map),
            pl.BlockSpec((1, 1, MOE_FA), wg_map),
            pl.BlockSpec((1, 1, MOE_FA), wl_map),
            pl.BlockSpec((1, 1, MOE_FB), wd_map),
        ],
        out_specs=pl.BlockSpec(memory_space=pl.ANY),
        scratch_shapes=[
            pltpu.VMEM((MOE_MAXM, D_MODEL), BF16),
            pltpu.VMEM((nfa, MOE_MAXM, MOE_FA), BF16),
            pltpu.VMEM((2, MOE_GATHER * STAGE_PITCH, LANES), F32),
            pltpu.VMEM((D_MODEL, 2 * MOE_FA), BF16),
            pltpu.VMEM((D_FF, MOE_FB), BF16),
            pltpu.VMEM((2, MOE_MAXM, MOE_FB), F32),
            pltpu.SMEM((2,), jnp.int32),
            pltpu.SemaphoreType.DMA((2,)),
            pltpu.SemaphoreType.DMA((2,)),
        ],
    )
    return pl.pallas_call(
        _moe_kernel,
        grid_spec=grid_spec,
        out_shape=jax.ShapeDtypeStruct((cap, D_MODEL), F32),
        compiler_params=_cparams(("arbitrary", "arbitrary")),
        name="moe",
    )(seg_e, seg_ep, seg_en, seg_cn, seg_row0, seg_tok0, seg_ngran, total_gran, tok_sorted,
      h2, w_gate_up, w_gate_up, w_down,
      b_gate_up.reshape(N_EXPERTS, 1, 2 * D_FF), b_gate_up.reshape(N_EXPERTS, 1, 2 * D_FF),
      b_down.reshape(N_EXPERTS, 1, D_MODEL))


def _combine_kernel(pos_ref, y_hbm, x1_ref, gate_ref, gfin_ref, op_ref, os_ref, buf, sem,
                    *, n_prompt_tiles):
    i = pl.program_id(0)
    n = pl.num_programs(0)
    tm = ROW_TM

    def row_copy(tile, slot, r, k):
        p = pos_ref[(tile * tm + r) * TOP_K + k]
        return pltpu.make_async_copy(y_hbm.at[pl.ds(p, 1)], buf.at[slot, k, pl.ds(r, 1)],
                                     sem.at[slot])

    def issue(tile, slot):
        def body(r, carry):
            for k in range(TOP_K):
                row_copy(tile, slot, r, k).start()
            return carry
        lax.fori_loop(0, tm, body, 0, unroll=4)

    def wait(slot):
        for k in range(TOP_K):
            pltpu.make_async_copy(y_hbm.at[pl.ds(0, tm)], buf.at[slot, k], sem.at[slot]).wait()

    slot = i % 2

    @pl.when(i == 0)
    def _():
        issue(0, 0)

    for parity in range(2):
        @pl.when(jnp.logical_and(i + 1 < n, slot == parity))
        def _():
            issue(i + 1, 1 - parity)

    wait(slot)
    gate = gate_ref[...]
    x2 = x1_ref[...]
    for k in range(TOP_K):
        x2 = x2 + gate[:, k:k + 1] * buf[slot, k]
    ms = jnp.mean(x2 * x2, axis=-1, keepdims=True)
    y = x2 * lax.rsqrt(ms + EPS) * gfin_ref[...]

    @pl.when(i < n_prompt_tiles)
    def _():
        op_ref[...] = y

    @pl.when(i >= n_prompt_tiles)
    def _():
        os_ref[...] = y


def _combine(pos, y_sorted, x1, gate, g_final, n_prompt, n_sample):
    tm = ROW_TM
    n_tok = x1.shape[0]
    n_prompt_tiles = n_prompt // tm
    n_sample_tiles = n_sample // tm

    grid_spec = pltpu.PrefetchScalarGridSpec(
        num_scalar_prefetch=1,
        grid=(n_tok // tm,),
        in_specs=[
            pl.BlockSpec(memory_space=pl.ANY),
            pl.BlockSpec((tm, D_MODEL), lambda i, pos: (i, 0)),
            pl.BlockSpec((tm, TOP_K), lambda i, pos: (i, 0)),
            pl.BlockSpec((1, D_MODEL), lambda i, pos: (0, 0)),
        ],
        out_specs=[
            pl.BlockSpec((tm, D_MODEL), lambda i, pos: (jnp.minimum(i, n_prompt_tiles - 1), 0)),
            pl.BlockSpec((tm, D_MODEL),
                         lambda i, pos: (jnp.clip(i - n_prompt_tiles, 0, n_sample_tiles - 1), 0)),
        ],
        scratch_shapes=[
            pltpu.VMEM((2, TOP_K, tm, D_MODEL), F32),
            pltpu.SemaphoreType.DMA((2,)),
        ],
    )
    return pl.pallas_call(
        functools.partial(_combine_kernel, n_prompt_tiles=n_prompt_tiles),
        grid_spec=grid_spec,
        out_shape=[
            jax.ShapeDtypeStruct((n_prompt, D_MODEL), F32),
            jax.ShapeDtypeStruct((n_sample, D_MODEL), F32),
        ],
        compiler_params=_cparams(("arbitrary",)),
        name="combine",
    )(pos, y_sorted, x1, gate, g_final.reshape(1, D_MODEL))


def kernel(x_prompt, x_sample, cache_k, cache_v, state_pool, meta_tokens, g_mix, w_in, sinks,
           w_pool_mix, pool_scale, w_br_attn, w_br_pool, w_out, g_ffn, w_router, b_router,
           w_gate_up, b_gate_up, w_down, b_down, g_final):
    depth = w_in.shape[0]
    assert depth == 1, "single-layer step only"
    batch, seq, _ = x_prompt.shape
    dec_batch, dec_seq, _ = x_sample.shape
    n_prompt = batch * seq
    n_sample = dec_batch * dec_seq
    n_tok = n_prompt + n_sample
    assert seq % ATTN_TILE == 0 and n_prompt % ROW_TM == 0 and n_sample % ROW_TM == 0
    assert dec_seq == SUBLANES and dec_batch % SAMPLE_SEQS == 0

    assert n_prompt % PROJ_TM == 0 and n_sample + ATTN_TILE <= PROJ_TM
    xp = x_prompt.reshape(n_prompt, D_MODEL)
    xs = x_sample.reshape(n_sample, D_MODEL)
    pad = jnp.zeros((PROJ_TM - n_sample - N_META, D_MODEL), F32)
    x_tail = jnp.concatenate([xs, pad, meta_tokens.astype(F32)], axis=0)

    p_all = _inproj(_prenorm(xp, x_tail, g_mix[0]), w_in[0])

    ck = cache_k[0].reshape(dec_batch, WINDOW, KV_WIDTH)
    cv = cache_v[0].reshape(dec_batch, WINDOW, KV_WIDTH)
    state_pad = jnp.pad(state_pool[0], ((0, 0), (N_META - POOL_BUF, 0), (0, 0)))
    o_attn, z = _attn_prompt(p_all, sinks[0], batch, seq, n_tok)
    o_attn, z = _attn_sample(p_all, sinks[0], ck, cv, state_pad, o_attn, z, n_prompt, dec_batch,
                             dec_seq)

    wr = w_router[0]
    wr_hi = wr.astype(BF16)
    wr_lo = (wr - wr_hi.astype(F32)).astype(BF16)
    x1, h2, top_idx, gate, rank, counts = _finish(
        xp, xs, o_attn, z, p_all, w_pool_mix[0].astype(BF16), pool_scale[0],
        w_br_attn[0].astype(BF16), w_br_pool[0].astype(BF16), w_out[0].astype(BF16),
        g_ffn[0], jnp.concatenate([wr_hi, wr_lo], axis=1), b_router[0])

    n_assign = n_tok * TOP_K
    n_seg = N_EXPERTS + n_assign // MOE_MAXM
    cap = (n_assign // MOE_ROWS + N_EXPERTS) * MOE_ROWS
    (n_live, seg_e, seg_ep, seg_en, seg_cn, seg_row0, seg_tok0, seg_ngran, total_gran, tok_sorted,
     dest) = _route(top_idx, rank, counts, n_seg)
    y_sorted = _moe(h2, n_live, seg_e, seg_ep, seg_en, seg_cn, seg_row0, seg_tok0, seg_ngran, total_gran,
                    tok_sorted, w_gate_up[0], b_gate_up[0], w_down[0], b_down[0], cap)
    y_p, y_s = _combine(dest, y_sorted, x1, gate, g_final, n_prompt, n_sample)

    k0, v0, u0, u1 = ATTN_WIDTH, ATTN_WIDTH + KV_WIDTH, ATTN_WIDTH + 2 * KV_WIDTH, IN_COLS - 2 * D_MODEL

    def tail_rows(n, c0, c1):
        return jnp.stack([p_all[(b + 1) * seq - n:(b + 1) * seq, c0:c1] for b in range(batch)])

    new_k_p = tail_rows(WINDOW, k0, v0).reshape(1, batch, WINDOW, N_KV_HEADS, HEAD_DIM)
    new_v_p = tail_rows(WINDOW, v0, u0).reshape(1, batch, WINDOW, N_KV_HEADS, HEAD_DIM)
    new_u_p = tail_rows(POOL_BUF, u0, u1)[None]
    ps = p_all[n_prompt:n_tok, k0:u1].reshape(dec_batch, dec_seq, u1 - k0)
    new_k_s = jnp.concatenate([ck[:, dec_seq:], ps[:, :, :KV_WIDTH]], axis=1).reshape(
        1, dec_batch, WINDOW, N_KV_HEADS, HEAD_DIM)
    new_v_s = jnp.concatenate([cv[:, dec_seq:], ps[:, :, KV_WIDTH:2 * KV_WIDTH]], axis=1).reshape(
        1, dec_batch, WINDOW, N_KV_HEADS, HEAD_DIM)
    new_u_s = jnp.concatenate([state_pool[0][:, dec_seq:], ps[:, :, 2 * KV_WIDTH:]], axis=1)[None]

    return (y_p.reshape(batch, seq, D_MODEL), y_s.reshape(dec_batch, dec_seq, D_MODEL),
            new_k_p, new_v_p, new_u_p, new_k_s, new_v_s, new_u_s)
```

```python
import functools

import jax
import jax.numpy as jnp
from jax import lax
from jax.experimental import pallas as pl
from jax.experimental.pallas import tpu as pltpu

F32 = jnp.float32
BF16 = jnp.bfloat16

D_MODEL = 2048
N_META = 16
N_HEADS = 32
N_KV_HEADS = 8
HEAD_DIM = 64
GROUP = N_HEADS // N_KV_HEADS
WINDOW = 128
ATTN_WIDTH = N_HEADS * HEAD_DIM
KV_WIDTH = N_KV_HEADS * HEAD_DIM
POOL_WIDTH = D_MODEL // 2
POOL_WINDOWS = (2, 4, 8, 16)
POOL_GROUP_DIM = POOL_WIDTH // len(POOL_WINDOWS)
POOL_BUF = max(POOL_WINDOWS) - 1
N_EXPERTS = 32
TOP_K = 4
D_FF = D_MODEL
SWIGLU_LIMIT = 7.0
SWIGLU_ALPHA = 1.702
EPS = 1e-5
IN_COLS = ATTN_WIDTH + 2 * KV_WIDTH + POOL_WIDTH + 2 * D_MODEL

LANES = 128
SUBLANES = 8
VMEM_LIMIT_BYTES = 58 * 1024 * 1024

ATTN_TILE = WINDOW
PROJ_TM = 512
PROJ_TN = 2048
ROW_TM = 256
SAMPLE_SEQS = 4
MOE_ROWS = 128
MOE_SUB = 2 * MOE_ROWS
MOE_MAXM = 12 * MOE_ROWS
MOE_FA = 512
MOE_FB = 512
MOE_GATHER = 2 * MOE_ROWS
MOE_KSLABS = 4
MOE_NFA = D_FF // MOE_FA
MOE_NFB = D_MODEL // MOE_FB
STAGE_PITCH = 3 * SUBLANES
NEG_BIG = -1e30

_SLOPES = tuple(float(2.0 ** (-8.0 * (i + 1) / N_HEADS)) for i in range(N_HEADS))


def _cparams(sem, vmem=VMEM_LIMIT_BYTES):
    return pltpu.CompilerParams(dimension_semantics=sem, vmem_limit_bytes=vmem)


def _rmsnorm(x, g):
    ms = jnp.mean(x * x, axis=-1, keepdims=True)
    return x * lax.rsqrt(ms + EPS) * g


def _prenorm_kernel(xp_ref, xt_ref, g_ref, h_ref, *, n_main):
    x = jnp.where(pl.program_id(0) < n_main, xp_ref[...], xt_ref[...])
    h_ref[...] = _rmsnorm(x, g_ref[...]).astype(h_ref.dtype)


def _prenorm(x_main, x_tail, g_mix):
    n_main = x_main.shape[0] // PROJ_TM
    n_rows = x_main.shape[0] + x_tail.shape[0]
    return pl.pallas_call(
        functools.partial(_prenorm_kernel, n_main=n_main),
        grid=(n_main + 1,),
        in_specs=[
            pl.BlockSpec((PROJ_TM, D_MODEL), lambda m: (jnp.minimum(m, n_main - 1), 0)),
            pl.BlockSpec((PROJ_TM, D_MODEL), lambda m: (0, 0)),
            pl.BlockSpec((1, D_MODEL), lambda m: (0, 0)),
        ],
        out_specs=pl.BlockSpec((PROJ_TM, D_MODEL), lambda m: (m, 0)),
        out_shape=jax.ShapeDtypeStruct((n_rows, D_MODEL), BF16),
        compiler_params=_cparams(("arbitrary",)),
        name="prenorm",
    )(x_main, x_tail, g_mix.reshape(1, D_MODEL))


def _inproj_kernel(h_ref, w_ref, o_ref, wbf_ref):
    @pl.when(pl.program_id(1) == 0)
    def _():
        wbf_ref[...] = w_ref[...].astype(BF16)

    o_ref[...] = jnp.dot(h_ref[...], wbf_ref[...], preferred_element_type=F32)


def _inproj(h, w_in):
    n_rows = h.shape[0]
    grid = (IN_COLS // PROJ_TN, n_rows // PROJ_TM)
    return pl.pallas_call(
        _inproj_kernel,
        grid=grid,
        in_specs=[
            pl.BlockSpec((PROJ_TM, D_MODEL), lambda n, m: (m, 0)),
            pl.BlockSpec((D_MODEL, PROJ_TN), lambda n, m: (0, n)),
        ],
        out_specs=pl.BlockSpec((PROJ_TM, PROJ_TN), lambda n, m: (m, n)),
        out_shape=jax.ShapeDtypeStruct((n_rows, IN_COLS), F32),
        scratch_shapes=[pltpu.VMEM((D_MODEL, PROJ_TN), BF16)],
        compiler_params=_cparams(("arbitrary", "arbitrary")),
        name="inproj",
    )(h, w_in)


def _attn_bias(n_q, n_keys, first_key):
    r = jnp.arange(n_q, dtype=jnp.int32)[:, None]
    c = jnp.arange(n_keys, dtype=jnp.int32)[None, :]
    dist = r + WINDOW - c
    valid = (dist >= 0) & (dist <= WINDOW) & (c >= first_key)
    slopes = jnp.asarray(_SLOPES, F32).reshape(N_KV_HEADS, GROUP, 1, 1)
    bias = jnp.where(valid[None, None], -slopes * dist.astype(F32)[None, None], NEG_BIG)
    return bias.reshape(N_KV_HEADS, GROUP * n_q, n_keys)


def _group_sinks(sink_ref, kvh, n_q):
    g_row = lax.broadcasted_iota(jnp.int32, (GROUP * n_q, 1), 0) // n_q
    sink = jnp.zeros((GROUP * n_q, 1), F32)
    for g in range(GROUP):
        sink = jnp.where(g_row == g, sink_ref[kvh * GROUP + g], sink)
    return sink


def _group_attention(q, k, v, bias, sink_ref, kvh, n_q):
    kh = k[:, kvh * HEAD_DIM:(kvh + 1) * HEAD_DIM]
    vh = v[:, kvh * HEAD_DIM:(kvh + 1) * HEAD_DIM]
    heads = [kvh * GROUP + g for g in range(GROUP)]
    qg = jnp.concatenate([q[:, hd * HEAD_DIM:(hd + 1) * HEAD_DIM] for hd in heads], axis=0)
    s = lax.dot_general(qg, kh, (((1,), (1,)), ((), ())), preferred_element_type=F32) + bias
    sink = _group_sinks(sink_ref, kvh, n_q)
    m = jnp.maximum(jnp.max(s, axis=-1, keepdims=True), sink)
    p = jnp.exp(s - m)
    den = jnp.sum(p, axis=-1, keepdims=True) + jnp.exp(sink - m)
    o = jnp.dot(p.astype(BF16), vh, preferred_element_type=F32)
    return o / den


def _pool_features(ext, n_halo):
    outs = []
    for g, w in enumerate(POOL_WINDOWS):
        e = ext[:, g * POOL_GROUP_DIM:(g + 1) * POOL_GROUP_DIM]
        s = e
        shift = 1
        while shift < w:
            s = s + pltpu.roll(s, shift, axis=0)
            shift *= 2
        outs.append(s[n_halo:] * (1.0 / w) - e[n_halo:])
    return jnp.concatenate(outs, axis=1)


def _attn_prompt_kernel(sink_ref, bias_ref, q_ref, ko_ref, kp_ref, vo_ref, vp_ref, uo_ref, up_ref,
                        o_ref, z_ref, s_scr, p_scr, *, n_prompt_tiles):
    t = ATTN_TILE
    step = pl.program_id(0)

    @pl.when(step < n_prompt_tiles)
    def _():
        q = (q_ref[...] * (HEAD_DIM ** -0.5)).astype(BF16)
        k = jnp.concatenate([kp_ref[...], ko_ref[...]], axis=0).astype(BF16)
        v = jnp.concatenate([vp_ref[...], vo_ref[...]], axis=0).astype(BF16)
        for kvh in range(N_KV_HEADS):
            kh = k[:, kvh * HEAD_DIM:(kvh + 1) * HEAD_DIM]
            qg = jnp.concatenate([q[:, (kvh * GROUP + g) * HEAD_DIM:(kvh * GROUP + g + 1) * HEAD_DIM]
                                  for g in range(GROUP)], axis=0)
            s_scr[kvh] = lax.dot_general(qg, kh, (((1,), (1,)), ((), ())),
                                         preferred_element_type=F32) + bias_ref[0, kvh]
        for kvh in range(N_KV_HEADS):
            s = s_scr[kvh]
            sink = _group_sinks(sink_ref, kvh, t)
            m = jnp.maximum(jnp.max(s, axis=-1, keepdims=True), sink)
            p = jnp.exp(s - m)
            den = jnp.sum(p, axis=-1, keepdims=True) + jnp.exp(sink - m)
            p_scr[kvh] = (p / den).astype(BF16)
        for kvh in range(N_KV_HEADS):
            vh = v[:, kvh * HEAD_DIM:(kvh + 1) * HEAD_DIM]
            o = jnp.dot(p_scr[kvh], vh, preferred_element_type=F32)
            for g in range(GROUP):
                hd = kvh * GROUP + g
                o_ref[:, hd * HEAD_DIM:(hd + 1) * HEAD_DIM] = o[g * t:(g + 1) * t].astype(o_ref.dtype)

        ext = jnp.concatenate([up_ref[...], uo_ref[...]], axis=0)
        z_ref[...] = _pool_features(ext, N_META).astype(z_ref.dtype)

    @pl.when(step >= n_prompt_tiles)
    def _():
        o_ref[...] = jnp.zeros_like(o_ref)
        z_ref[...] = jnp.zeros_like(z_ref)


def _attn_prompt(p_all, sinks, batch, seq, n_out_rows):
    t = ATTN_TILE
    tiles = seq // t
    n_prompt_tiles = batch * tiles
    assert n_out_rows % t == 0
    meta_blk = p_all.shape[0] // t - 1
    q_w, kv_w, u_w = ATTN_WIDTH, KV_WIDTH, POOL_WIDTH
    kcol, vcol, ucol = ATTN_WIDTH // kv_w, ATTN_WIDTH // kv_w + 1, (ATTN_WIDTH + 2 * kv_w) // u_w
    halo_per_tile = t // N_META
    bias = jnp.stack([_attn_bias(t, 2 * t, t - N_META), _attn_bias(t, 2 * t, 0)])

    def prev(s):
        return jnp.where(s % tiles > 0, s - 1, meta_blk)

    return pl.pallas_call(
        functools.partial(_attn_prompt_kernel, n_prompt_tiles=n_prompt_tiles),
        grid=(n_out_rows // t,),
        in_specs=[
            pl.BlockSpec(memory_space=pltpu.SMEM),
            pl.BlockSpec((1, N_KV_HEADS, GROUP * t, 2 * t), lambda s: (jnp.minimum(s % tiles, 1), 0, 0, 0)),
            pl.BlockSpec((t, q_w), lambda s: (s, 0)),
            pl.BlockSpec((t, kv_w), lambda s: (s, kcol)),
            pl.BlockSpec((t, kv_w), lambda s: (prev(s), kcol)),
            pl.BlockSpec((t, kv_w), lambda s: (s, vcol)),
            pl.BlockSpec((t, kv_w), lambda s: (prev(s), vcol)),
            pl.BlockSpec((t, u_w), lambda s: (s, ucol)),
            pl.BlockSpec((N_META, u_w), lambda s: (prev(s) * halo_per_tile + halo_per_tile - 1, ucol)),
        ],
        out_specs=[
            pl.BlockSpec((t, q_w), lambda s: (s, 0)),
            pl.BlockSpec((t, u_w), lambda s: (s, 0)),
        ],
        out_shape=[
            jax.ShapeDtypeStruct((n_out_rows, q_w), BF16),
            jax.ShapeDtypeStruct((n_out_rows, u_w), BF16),
        ],
        scratch_shapes=[
            pltpu.VMEM((N_KV_HEADS, GROUP * t, 2 * t), F32),
            pltpu.VMEM((N_KV_HEADS, GROUP * t, 2 * t), BF16),
        ],
        compiler_params=_cparams(("arbitrary",)),
        name="attn_prompt",
    )(sinks, bias, p_all, p_all, p_all, p_all, p_all, p_all, p_all)


def _attn_sample_kernel(sink_ref, bias_ref, q_ref, kn_ref, vn_ref, un_ref, ck_ref, cv_ref, sp_ref,
                        o_prev_ref, z_prev_ref, o_ref, z_ref):
    del o_prev_ref, z_prev_ref
    n_seqs = ck_ref.shape[0]
    n_new = q_ref.shape[0] // n_seqs
    q_all = (q_ref[...] * (HEAD_DIM ** -0.5)).astype(BF16)
    for si in range(n_seqs):
        rows = slice(si * n_new, (si + 1) * n_new)
        q = q_all[rows]
        k = jnp.concatenate([ck_ref[si], kn_ref[rows, :]], axis=0).astype(BF16)
        v = jnp.concatenate([cv_ref[si], vn_ref[rows, :]], axis=0).astype(BF16)
        for kvh in range(N_KV_HEADS):
            o = _group_attention(q, k, v, bias_ref[kvh], sink_ref, kvh, n_new)
            for g in range(GROUP):
                hd = kvh * GROUP + g
                o_ref[rows, hd * HEAD_DIM:(hd + 1) * HEAD_DIM] = (
                    o[g * n_new:(g + 1) * n_new].astype(o_ref.dtype))
        ext = jnp.concatenate([sp_ref[si], un_ref[rows, :]], axis=0)
        z_ref[rows, :] = _pool_features(ext, sp_ref.shape[1]).astype(z_ref.dtype)


def _attn_sample(p_all, sinks, cache_k, cache_v, state_pad, o_attn, z, row0, dec_batch, dec_seq):
    q_w, kv_w, u_w = ATTN_WIDTH, KV_WIDTH, POOL_WIDTH
    kcol, vcol, ucol = ATTN_WIDTH // kv_w, ATTN_WIDTH // kv_w + 1, (ATTN_WIDTH + 2 * kv_w) // u_w
    ns = SAMPLE_SEQS
    rows = ns * dec_seq
    blk0 = row0 // rows
    n_halo = state_pad.shape[1]
    n_keys = WINDOW + dec_seq
    bias = _attn_bias(dec_seq, n_keys, 0)
    return pl.pallas_call(
        _attn_sample_kernel,
        grid=(dec_batch // ns,),
        in_specs=[
            pl.BlockSpec(memory_space=pltpu.SMEM),
            pl.BlockSpec((N_KV_HEADS, GROUP * dec_seq, n_keys), lambda b: (0, 0, 0)),
            pl.BlockSpec((rows, q_w), lambda b: (blk0 + b, 0)),
            pl.BlockSpec((rows, kv_w), lambda b: (blk0 + b, kcol)),
            pl.BlockSpec((rows, kv_w), lambda b: (blk0 + b, vcol)),
            pl.BlockSpec((rows, u_w), lambda b: (blk0 + b, ucol)),
            pl.BlockSpec((ns, WINDOW, kv_w), lambda b: (b, 0, 0)),
            pl.BlockSpec((ns, WINDOW, kv_w), lambda b: (b, 0, 0)),
            pl.BlockSpec((ns, n_halo, u_w), lambda b: (b, 0, 0)),
            pl.BlockSpec(memory_space=pl.ANY),
            pl.BlockSpec(memory_space=pl.ANY),
        ],
        out_specs=[
            pl.BlockSpec((rows, q_w), lambda b: (blk0 + b, 0)),
            pl.BlockSpec((rows, u_w), lambda b: (blk0 + b, 0)),
        ],
        out_shape=[
            jax.ShapeDtypeStruct(o_attn.shape, o_attn.dtype),
            jax.ShapeDtypeStruct(z.shape, z.dtype),
        ],
        input_output_aliases={9: 0, 10: 1},
        compiler_params=_cparams(("arbitrary",)),
        name="attn_sample",
    )(sinks, bias, p_all, p_all, p_all, p_all, cache_k, cache_v, state_pad, o_attn, z)


def _finish_kernel(xp_ref, xs_ref, oa_ref, z_ref, ag_ref, pg_ref, wpm_ref, ps_ref, wba_ref, wbp_ref,
                   wo_ref, gf_ref, wr_ref, br_ref,
                   x1_ref, h2_ref, idx_ref, gate_ref, rank_ref, cnt_ref, carry_ref,
                   *, n_prompt_tiles):
    tm = xp_ref.shape[0]
    i = pl.program_id(0)

    @pl.when(i == 0)
    def _():
        carry_ref[...] = jnp.zeros_like(carry_ref)

    x_in = jnp.where(i < n_prompt_tiles, xp_ref[...], xs_ref[...])

    z = z_ref[...]
    zp = jnp.concatenate(
        [jnp.dot(z[:, g * POOL_GROUP_DIM:(g + 1) * POOL_GROUP_DIM], wpm_ref[g],
                 preferred_element_type=F32) for g in range(len(POOL_WINDOWS))], axis=1)
    zp = (zp * ps_ref[...]).astype(BF16)
    ya = jnp.dot(oa_ref[...], wba_ref[...], preferred_element_type=F32)
    yp = jnp.dot(zp, wbp_ref[...], preferred_element_type=F32)
    merged = jax.nn.sigmoid(ag_ref[...]) * ya + jax.nn.sigmoid(pg_ref[...]) * yp
    x1 = x_in + jnp.dot(merged.astype(BF16), wo_ref[...], preferred_element_type=F32)
    x1_ref[...] = x1
    h2 = _rmsnorm(x1, gf_ref[...])
    n_lc = D_MODEL // LANES
    for c in range(n_lc):
        h2_ref[pl.ds(c, tm, stride=n_lc), :] = h2[:, c * LANES:(c + 1) * LANES]

    h_hi = h2.astype(BF16)
    h_lo = (h2 - h_hi.astype(F32)).astype(BF16)
    t = jnp.dot(h_hi, wr_ref[...], preferred_element_type=F32)
    logits = (t[:, :N_EXPERTS] + t[:, N_EXPERTS:]
              + jnp.dot(h_lo, wr_ref[:, :N_EXPERTS], preferred_element_type=F32) + br_ref[...])

    col = lax.broadcasted_iota(jnp.int32, logits.shape, 1).astype(F32)
    vals, idxs = [], []
    for _ in range(TOP_K):
        m = jnp.max(logits, axis=-1, keepdims=True)
        idx = jnp.min(jnp.where(logits == m, col, float(N_EXPERTS)), axis=-1, keepdims=True)
        vals.append(m)
        idxs.append(idx)
        logits = jnp.where(col == idx, -jnp.inf, logits)
    exps = [jnp.exp(v - vals[0]) for v in vals]
    den = exps[0] + exps[1] + exps[2] + exps[3]

    member = jnp.zeros(logits.shape, F32)
    for k in range(TOP_K):
        member = member + (col == idxs[k]).astype(F32)
    ri = lax.broadcasted_iota(jnp.int32, (tm, tm), 0)
    ci = lax.broadcasted_iota(jnp.int32, (tm, tm), 1)
    earlier = (ri > ci).astype(BF16)
    before = jnp.dot(earlier, member.astype(BF16), preferred_element_type=F32) + carry_ref[...]
    for k in range(TOP_K):
        idx_ref[:, k:k + 1] = idxs[k].astype(jnp.int32)
        gate_ref[:, k:k + 1] = exps[k] / den
        rank_ref[:, k:k + 1] = jnp.sum(jnp.where(col == idxs[k], before, 0.0), axis=-1,
                                       keepdims=True).astype(jnp.int32)
    carry_ref[...] += jnp.sum(member, axis=0, keepdims=True)
    cnt_ref[...] = carry_ref[...]


def _finish(x_prompt, x_sample, o_attn, z, p_all, wpm, pool_scale, wba, wbp, wo, g_ffn, w_router2,
            b_router):
    n_tok = o_attn.shape[0]
    tm = ROW_TM
    n_prompt_tiles = x_prompt.shape[0] // tm
    n_sample_tiles = x_sample.shape[0] // tm
    acol, pcol = (IN_COLS - 2 * D_MODEL) // D_MODEL, (IN_COLS - D_MODEL) // D_MODEL
    const = pl.Buffered(1)

    def whole(shape):
        nd = len(shape)
        return pl.BlockSpec(shape, lambda i: (0,) * nd, pipeline_mode=const)

    return pl.pallas_call(
        functools.partial(_finish_kernel, n_prompt_tiles=n_prompt_tiles),
        grid=(n_tok // tm,),
        in_specs=[
            pl.BlockSpec((tm, D_MODEL), lambda i: (jnp.minimum(i, n_prompt_tiles - 1), 0)),
            pl.BlockSpec((tm, D_MODEL),
                         lambda i: (jnp.clip(i - n_prompt_tiles, 0, n_sample_tiles - 1), 0)),
            pl.BlockSpec((tm, ATTN_WIDTH), lambda i: (i, 0)),
            pl.BlockSpec((tm, POOL_WIDTH), lambda i: (i, 0)),
            pl.BlockSpec((tm, D_MODEL), lambda i: (i, acol)),
            pl.BlockSpec((tm, D_MODEL), lambda i: (i, pcol)),
            whole(wpm.shape), whole((1, POOL_WIDTH)), whole(wba.shape), whole(wbp.shape),
            whole(wo.shape), whole((1, D_MODEL)), whole(w_router2.shape), whole((1, N_EXPERTS)),
        ],
        out_specs=[
            pl.BlockSpec((tm, D_MODEL), lambda i: (i, 0)),
            pl.BlockSpec((tm * (D_MODEL // LANES), LANES), lambda i: (i, 0)),
            pl.BlockSpec((tm, TOP_K), lambda i: (i, 0)),
            pl.BlockSpec((tm, TOP_K), lambda i: (i, 0)),
            pl.BlockSpec((tm, TOP_K), lambda i: (i, 0)),
            pl.BlockSpec((1, N_EXPERTS), lambda i: (0, 0)),
        ],
        out_shape=[
            jax.ShapeDtypeStruct((n_tok, D_MODEL), F32),
            jax.ShapeDtypeStruct((n_tok * (D_MODEL // LANES), LANES), F32),
            jax.ShapeDtypeStruct((n_tok, TOP_K), jnp.int32),
            jax.ShapeDtypeStruct((n_tok, TOP_K), F32),
            jax.ShapeDtypeStruct((n_tok, TOP_K), jnp.int32),
            jax.ShapeDtypeStruct((1, N_EXPERTS), F32),
        ],
        scratch_shapes=[pltpu.VMEM((1, N_EXPERTS), F32)],
        compiler_params=_cparams(("arbitrary",)),
        name="finish",
    )(x_prompt, x_sample, o_attn, z, p_all, p_all, wpm, pool_scale.reshape(1, POOL_WIDTH), wba, wbp,
      wo, g_ffn.reshape(1, D_MODEL), w_router2, b_router.reshape(1, N_EXPERTS))


def _route(top_idx, rank, counts_f, n_seg):
    counts = counts_f.reshape(N_EXPERTS).astype(jnp.int32)
    padded = (counts + MOE_ROWS - 1) // MOE_ROWS * MOE_ROWS
    pend = jnp.cumsum(padded)
    pstart = pend - padded
    experts = jnp.arange(N_EXPERTS, dtype=jnp.int32)
    start_of = jnp.sum(jnp.where(top_idx[..., None] == experts, pstart, 0), axis=-1)
    dest = (start_of + rank).reshape(-1)
    _, tok_order = lax.sort_key_val(dest, jnp.arange(dest.shape[0], dtype=jnp.int32) // TOP_K)
    tok_sorted = jnp.concatenate([tok_order, jnp.zeros((MOE_GATHER,), jnp.int32)])
    cstart = jnp.cumsum(counts) - counts

    nseg_e = (counts + MOE_MAXM - 1) // MOE_MAXM
    seg_end = jnp.cumsum(nseg_e)
    seg_base = seg_end - nseg_e
    s_ids = jnp.arange(n_seg, dtype=jnp.int32)
    last = jnp.maximum(seg_end[-1] - 1, 0)
    s_eff = jnp.minimum(s_ids, last)
    e_of_s = jnp.minimum(jnp.sum(s_eff[:, None] >= seg_end[None, :], axis=1), N_EXPERTS - 1)
    e_prev = jnp.concatenate([e_of_s[:1], e_of_s[:-1]])
    is_last = s_ids >= last
    e_next = jnp.where(is_last, e_of_s, jnp.concatenate([e_of_s[1:], e_of_s[-1:]]))
    c_next = jnp.where(is_last, MOE_NFA - 1, 0)
    k_in = s_eff - seg_base[e_of_s]
    nrows = jnp.clip(counts[e_of_s] - k_in * MOE_MAXM, 0, MOE_MAXM)
    n_gran = (nrows + MOE_ROWS - 1) // MOE_ROWS
    row0 = pstart[e_of_s] + k_in * MOE_MAXM
    tok0 = cstart[e_of_s] + k_in * MOE_MAXM
    total_gran = (pend[-1] // MOE_ROWS).reshape(1)
    i32 = jnp.int32
    n_live = (last + 1).astype(i32)
    return (n_live, e_of_s.astype(i32), e_prev.astype(i32), e_next.astype(i32), c_next.astype(i32),
            row0.astype(i32), tok0.astype(i32), n_gran.astype(i32), total_gran.astype(i32), tok_sorted,
            dest.astype(i32))


def _for_blocks(base, n_gran, fn):
    n_full = n_gran // 2

    def pair(p, carry):
        r = base + p * (2 * MOE_SUB)
        fn(r, MOE_SUB)
        fn(r + MOE_SUB, MOE_SUB)
        return carry
    lax.fori_loop(0, n_full // 2, pair, 0)

    @pl.when(n_full % 2 == 1)
    def _():
        fn(base + (n_full - 1) * MOE_SUB, MOE_SUB)

    @pl.when(n_gran % 2 == 1)
    def _():
        fn(base + n_full * MOE_SUB, MOE_ROWS)


def _moe_kernel(seg_e, seg_ep, seg_en, seg_cn, seg_row0, seg_tok0, seg_ngran, total_gran, tok_smem,
                h2_hbm, wg_ref, wl_ref, wd_ref, bg_ref, bl_ref, bd_ref,
                y_hbm,
                xbuf, actbuf, stage, wab, wdb, ostage, pend, gsem, osem):
    del seg_e, seg_ep, seg_en, seg_cn
    s = pl.program_id(0)
    j = pl.program_id(1)
    n_gran = seg_ngran[s]
    row0 = seg_row0[s]
    live = n_gran > 0
    gran = MOE_ROWS
    fa, fb = MOE_FA, MOE_FB
    n_lc = D_MODEL // LANES
    kslab = D_MODEL // MOE_KSLABS

    def rows_at(first, n_rows):
        return pl.ds(pl.multiple_of(first, MOE_ROWS), n_rows)

    gch = MOE_GATHER

    def issue(tok0, chunk, slot):
        base = tok0 + chunk * gch

        def body(r, carry):
            tok = tok_smem[base + r]
            pltpu.make_async_copy(
                h2_hbm.at[pl.ds(pl.multiple_of(tok * n_lc, n_lc), n_lc)],
                stage.at[slot, pl.ds(pl.multiple_of(r * STAGE_PITCH, SUBLANES), n_lc)],
                gsem.at[slot]).start(priority=1)
            return carry
        lax.fori_loop(0, gch, body, 0, unroll=8)

    def land(chunk, slot):
        pltpu.make_async_copy(h2_hbm.at[pl.ds(0, gch * n_lc)], stage.at[slot, pl.ds(0, gch * n_lc)],
                              gsem.at[slot]).wait()
        rows = rows_at(chunk * gch, gch)
        for lc in range(n_lc):
            xbuf[rows, lc * LANES:(lc + 1) * LANES] = (
                stage[slot, pl.ds(lc, gch, stride=STAGE_PITCH), :].astype(BF16))

    def n_chunks(n_gran_):
        return (n_gran_ * gran + gch - 1) // gch

    @pl.when(jnp.logical_and(s == 0, j == 0))
    def _first():
        pend[0] = 0
        pend[1] = 0

        nc = n_chunks(n_gran)
        issue(seg_tok0[s], 0, 0)

        def chunk_body(c, carry):
            slot = c % 2

            @pl.when(c + 1 < nc)
            def _():
                issue(seg_tok0[s], c + 1, 1 - slot)
            land(c, slot)
            return carry
        lax.fori_loop(0, nc, chunk_body, 0)

    def prefetch_next_rows(jb):
        has_next = s + 1 < pl.num_programs(0)
        s_next = jnp.minimum(s + 1, pl.num_programs(0) - 1)
        nc = jnp.where(has_next, n_chunks(seg_ngran[s_next]), 0)
        next_tok0 = seg_tok0[s_next]
        for q in range(2):
            @pl.when(jnp.logical_and(jb >= 1, 2 * (jb - 1) + q < nc))
            def _():
                land(2 * (jb - 1) + q, q)
        for q in range(2):
            @pl.when(jnp.logical_and(jb + 1 < MOE_NFB, 2 * jb + q < nc))
            def _():
                issue(next_tok0, 2 * jb + q, q)

    @pl.when(jnp.logical_and(j < MOE_NFA, live))
    def _up():
        bias = jnp.concatenate([bg_ref[0], bl_ref[0]], axis=1)

        def activate(rows, gu):
            glu = jnp.minimum(gu[:, :fa], SWIGLU_LIMIT)
            lin = jnp.clip(gu[:, fa:], -SWIGLU_LIMIT, SWIGLU_LIMIT)
            actbuf[j, rows, :] = (glu * jax.nn.sigmoid(SWIGLU_ALPHA * glu) * (lin + 1.0)).astype(BF16)

        rows0 = pl.ds(0, gran)
        gu = bias
        for c in range(MOE_KSLABS):
            ks = slice(c * kslab, (c + 1) * kslab)
            wab[ks, :fa] = wg_ref[0, ks, :].astype(BF16)
            wab[ks, fa:] = wl_ref[0, ks, :].astype(BF16)
            gu = gu + jnp.dot(xbuf[rows0, ks], wab[ks, :], preferred_element_type=F32)
        activate(rows0, gu)

        def one(first, n_rows):
            rows = rows_at(first, n_rows)
            activate(rows, jnp.dot(xbuf[rows, :], wab[...], preferred_element_type=F32) + bias)
        _for_blocks(gran, n_gran - 1, one)

    def out_copy(slot, src_row, dst_row, col0):
        return pltpu.make_async_copy(ostage.at[slot, rows_at(src_row, gran)],
                                     y_hbm.at[rows_at(dst_row, gran), pl.ds(col0, fb)], osem.at[slot])

    def out_wait(slot):
        def body(i, carry):
            out_copy(slot, 0, 0, 0).wait()
            return carry
        lax.fori_loop(0, pend[slot], body, 0)
        pend[slot] = 0

    @pl.when(jnp.logical_and(j >= MOE_NFA, live))
    def _down():
        jb = j - MOE_NFA
        slot = jb % 2
        col0 = pl.multiple_of(jb * fb, fb)
        prefetch_next_rows(jb)
        out_wait(slot)

        rows0 = pl.ds(0, gran)
        yd = bd_ref[0]
        for c in range(MOE_NFA):
            ks = slice(c * fa, (c + 1) * fa)
            wdb[ks, :] = wd_ref[0, ks, :].astype(BF16)
            yd = yd + jnp.dot(actbuf[c, rows0, :], wdb[ks, :], preferred_element_type=F32)
        ostage[slot, rows0, :] = yd
        out_copy(slot, 0, row0, col0).start()

        def one(first, n_rows):
            rows = rows_at(first, n_rows)
            act = jnp.concatenate([actbuf[c, rows, :] for c in range(MOE_NFA)], axis=1)
            ostage[slot, rows, :] = jnp.dot(act, wdb[...], preferred_element_type=F32) + bd_ref[0]
            for h in range(n_rows // gran):
                out_copy(slot, first + h * gran, row0 + first + h * gran, col0).start()
        _for_blocks(gran, n_gran - 1, one)
        pend[slot] = n_gran

    @pl.when(jnp.logical_and(s == pl.num_programs(0) - 1, j == pl.num_programs(1) - 1))
    def _final():
        out_wait(0)
        out_wait(1)
        n_tail = y_hbm.shape[0] // gran - total_gran[0]

        @pl.when(n_tail > 0)
        def _():
            ostage[0, pl.ds(0, gran), :] = jnp.zeros((gran, fb), F32)

            def fill(t, carry):
                for cb in range(MOE_NFB):
                    out_copy(0, 0, (total_gran[0] + t) * gran, cb * fb).start()
                return carry
            lax.fori_loop(0, n_tail, fill, 0)
            pend[0] = n_tail * MOE_NFB
            out_wait(0)


def _moe(h2, n_live, seg_e, seg_ep, seg_en, seg_cn, seg_row0, seg_tok0, seg_ngran, total_gran, tok_sorted,
         w_gate_up, b_gate_up, w_down, b_down, cap):
    nfa, nfb = MOE_NFA, MOE_NFB
    assert MOE_MAXM <= 2 * (nfb - 1) * MOE_GATHER

    def up_block(s, j, e, en, cn):
        ahead = j >= nfa + nfb // 2
        eb = jnp.where(ahead, en[s], e[s])
        cb = jnp.where(j < nfa, j, jnp.where(ahead, cn[s], nfa - 1))
        return eb, cb

    def wg_map(s, j, e, ep, en, cn, r0, t0, ng, tg, tok):
        eb, cb = up_block(s, j, e, en, cn)
        return (eb, 0, cb)

    def wl_map(s, j, e, ep, en, cn, r0, t0, ng, tg, tok):
        eb, cb = up_block(s, j, e, en, cn)
        return (eb, 0, nfa + cb)

    def wd_map(s, j, e, ep, en, cn, r0, t0, ng, tg, tok):
        in_up = j < nfa
        eb = jnp.where(in_up, ep[s], e[s])
        jb = jnp.where(in_up, jnp.where(s > 0, nfb - 1, 0), j - nfa)
        return (eb, 0, jb)

    grid_spec = pltpu.PrefetchScalarGridSpec(
        num_scalar_prefetch=9,
        grid=(n_live, nfa + nfb),
        in_specs=[
            pl.BlockSpec(memory_space=pl.ANY),
            pl.BlockSpec((1, D_MODEL, MOE_FA), wg_map),
            pl.BlockSpec((1, D_MODEL, MOE_FA), wl_map),
            pl.BlockSpec((1, D_FF, MOE_FB), wd_map),
            pl.BlockSpec((1, 1, MOE_FA), wg_map),
            pl.BlockSpec((1, 1, MOE_FA), wl_map),
            pl.BlockSpec((1, 1, MOE_FB), wd_map),
        ],
        out_specs=pl.BlockSpec(memory_space=pl.ANY),
        scratch_shapes=[
            pltpu.VMEM((MOE_MAXM, D_MODEL), BF16),
            pltpu.VMEM((nfa, MOE_MAXM, MOE_FA), BF16),
            pltpu.VMEM((2, MOE_GATHER * STAGE_PITCH, LANES), F32),
            pltpu.VMEM((D_MODEL, 2 * MOE_FA), BF16),
            pltpu.VMEM((D_FF, MOE_FB), BF16),
            pltpu.VMEM((2, MOE_MAXM, MOE_FB), F32),
            pltpu.SMEM((2,), jnp.int32),
            pltpu.SemaphoreType.DMA((2,)),
            pltpu.SemaphoreType.DMA((2,)),
        ],
    )
    return pl.pallas_call(
        _moe_kernel,
        grid_spec=grid_spec,
        out_shape=jax.ShapeDtypeStruct((cap, D_MODEL), F32),
        compiler_params=_cparams(("arbitrary", "arbitrary")),
        name="moe",
    )(seg_e, seg_ep, seg_en, seg_cn, seg_row0, seg_tok0, seg_ngran, total_gran, tok_sorted,
      h2, w_gate_up, w_gate_up, w_down,
      b_gate_up.reshape(N_EXPERTS, 1, 2 * D_FF), b_gate_up.reshape(N_EXPERTS, 1, 2 * D_FF),
      b_down.reshape(N_EXPERTS, 1, D_MODEL))


def _combine_kernel(pos_ref, y_hbm, x1_ref, gate_ref, gfin_ref, op_ref, os_ref, buf, sem,
                    *, n_prompt_tiles):
    i = pl.program_id(0)
    n = pl.num_programs(0)
    tm = ROW_TM

    def row_copy(tile, slot, r, k):
        p = pos_ref[(tile * tm + r) * TOP_K + k]
        return pltpu.make_async_copy(y_hbm.at[pl.ds(p, 1)], buf.at[slot, k, pl.ds(r, 1)],
                                     sem.at[slot])

    def issue(tile, slot):
        def body(r, carry):
            for k in range(TOP_K):
                row_copy(tile, slot, r, k).start()
            return carry
        lax.fori_loop(0, tm, body, 0, unroll=4)

    def wait(slot):
        for k in range(TOP_K):
            pltpu.make_async_copy(y_hbm.at[pl.ds(0, tm)], buf.at[slot, k], sem.at[slot]).wait()

    slot = i % 2

    @pl.when(i == 0)
    def _():
        issue(0, 0)

    for parity in range(2):
        @pl.when(jnp.logical_and(i + 1 < n, slot == parity))
        def _():
            issue(i + 1, 1 - parity)

    wait(slot)
    gate = gate_ref[...]
    x2 = x1_ref[...]
    for k in range(TOP_K):
        x2 = x2 + gate[:, k:k + 1] * buf[slot, k]
    ms = jnp.mean(x2 * x2, axis=-1, keepdims=True)
    y = x2 * lax.rsqrt(ms + EPS) * gfin_ref[...]

    @pl.when(i < n_prompt_tiles)
    def _():
        op_ref[...] = y

    @pl.when(i >= n_prompt_tiles)
    def _():
        os_ref[...] = y


def _combine(pos, y_sorted, x1, gate, g_final, n_prompt, n_sample):
    tm = ROW_TM
    n_tok = x1.shape[0]
    n_prompt_tiles = n_prompt // tm
    n_sample_tiles = n_sample // tm

    grid_spec = pltpu.PrefetchScalarGridSpec(
        num_scalar_prefetch=1,
        grid=(n_tok // tm,),
        in_specs=[
            pl.BlockSpec(memory_space=pl.ANY),
            pl.BlockSpec((tm, D_MODEL), lambda i, pos: (i, 0)),
            pl.BlockSpec((tm, TOP_K), lambda i, pos: (i, 0)),
            pl.BlockSpec((1, D_MODEL), lambda i, pos: (0, 0)),
        ],
        out_specs=[
            pl.BlockSpec((tm, D_MODEL), lambda i, pos: (jnp.minimum(i, n_prompt_tiles - 1), 0)),
            pl.BlockSpec((tm, D_MODEL),
                         lambda i, pos: (jnp.clip(i - n_prompt_tiles, 0, n_sample_tiles - 1), 0)),
        ],
        scratch_shapes=[
            pltpu.VMEM((2, TOP_K, tm, D_MODEL), F32),
            pltpu.SemaphoreType.DMA((2,)),
        ],
    )
    return pl.pallas_call(
        functools.partial(_combine_kernel, n_prompt_tiles=n_prompt_tiles),
        grid_spec=grid_spec,
        out_shape=[
            jax.ShapeDtypeStruct((n_prompt, D_MODEL), F32),
            jax.ShapeDtypeStruct((n_sample, D_MODEL), F32),
        ],
        compiler_params=_cparams(("arbitrary",)),
        name="combine",
    )(pos, y_sorted, x1, gate, g_final.reshape(1, D_MODEL))


def kernel(x_prompt, x_sample, cache_k, cache_v, state_pool, meta_tokens, g_mix, w_in, sinks,
           w_pool_mix, pool_scale, w_br_attn, w_br_pool, w_out, g_ffn, w_router, b_router,
           w_gate_up, b_gate_up, w_down, b_down, g_final):
    depth = w_in.shape[0]
    assert depth == 1, "single-layer step only"
    batch, seq, _ = x_prompt.shape
    dec_batch, dec_seq, _ = x_sample.shape
    n_prompt = batch * seq
    n_sample = dec_batch * dec_seq
    n_tok = n_prompt + n_sample
    assert seq % ATTN_TILE == 0 and n_prompt % ROW_TM == 0 and n_sample % ROW_TM == 0
    assert dec_seq == SUBLANES and dec_batch % SAMPLE_SEQS == 0

    assert n_prompt % PROJ_TM == 0 and n_sample + ATTN_TILE <= PROJ_TM
    xp = x_prompt.reshape(n_prompt, D_MODEL)
    xs = x_sample.reshape(n_sample, D_MODEL)
    pad = jnp.zeros((PROJ_TM - n_sample - N_META, D_MODEL), F32)
    x_tail = jnp.concatenate([xs, pad, meta_tokens.astype(F32)], axis=0)

    p_all = _inproj(_prenorm(xp, x_tail, g_mix[0]), w_in[0])

    ck = cache_k[0].reshape(dec_batch, WINDOW, KV_WIDTH)
    cv = cache_v[0].reshape(dec_batch, WINDOW, KV_WIDTH)
    state_pad = jnp.pad(state_pool[0], ((0, 0), (N_META - POOL_BUF, 0), (0, 0)))
    o_attn, z = _attn_prompt(p_all, sinks[0], batch, seq, n_tok)
    o_attn, z = _attn_sample(p_all, sinks[0], ck, cv, state_pad, o_attn, z, n_prompt, dec_batch,
                             dec_seq)

    wr = w_router[0]
    wr_hi = wr.astype(BF16)
    wr_lo = (wr - wr_hi.astype(F32)).astype(BF16)
    x1, h2, top_idx, gate, rank, counts = _finish(
        xp, xs, o_attn, z, p_all, w_pool_mix[0].astype(BF16), pool_scale[0],
        w_br_attn[0].astype(BF16), w_br_pool[0].astype(BF16), w_out[0].astype(BF16),
        g_ffn[0], jnp.concatenate([wr_hi, wr_lo], axis=1), b_router[0])

    n_assign = n_tok * TOP_K
    n_seg = N_EXPERTS + n_assign // MOE_MAXM
    cap = (n_assign // MOE_ROWS + N_EXPERTS) * MOE_ROWS
    (n_live, seg_e, seg_ep, seg_en, seg_cn, seg_row0, seg_tok0, seg_ngran, total_gran, tok_sorted,
     dest) = _route(top_idx, rank, counts, n_seg)
    y_sorted = _moe(h2, n_live, seg_e, seg_ep, seg_en, seg_cn, seg_row0, seg_tok0, seg_ngran, total_gran,
                    tok_sorted, w_gate_up[0], b_gate_up[0], w_down[0], b_down[0], cap)
    y_p, y_s = _combine(dest, y_sorted, x1, gate, g_final, n_prompt, n_sample)

    k0, v0, u0, u1 = ATTN_WIDTH, ATTN_WIDTH + KV_WIDTH, ATTN_WIDTH + 2 * KV_WIDTH, IN_COLS - 2 * D_MODEL

    def tail_rows(n, c0, c1):
        return jnp.stack([p_all[(b + 1) * seq - n:(b + 1) * seq, c0:c1] for b in range(batch)])

    new_k_p = tail_rows(WINDOW, k0, v0).reshape(1, batch, WINDOW, N_KV_HEADS, HEAD_DIM)
    new_v_p = tail_rows(WINDOW, v0, u0).reshape(1, batch, WINDOW, N_KV_HEADS, HEAD_DIM)
    new_u_p = tail_rows(POOL_BUF, u0, u1)[None]
    ps = p_all[n_prompt:n_tok, k0:u1].reshape(dec_batch, dec_seq, u1 - k0)
    new_k_s = jnp.concatenate([ck[:, dec_seq:], ps[:, :, :KV_WIDTH]], axis=1).reshape(
        1, dec_batch, WINDOW, N_KV_HEADS, HEAD_DIM)
    new_v_s = jnp.concatenate([cv[:, dec_seq:], ps[:, :, KV_WIDTH:2 * KV_WIDTH]], axis=1).reshape(
        1, dec_batch, WINDOW, N_KV_HEADS, HEAD_DIM)
    new_u_s = jnp.concatenate([state_pool[0][:, dec_seq:], ps[:, :, 2 * KV_WIDTH:]], axis=1)[None]

    return (y_p.reshape(batch, seq, D_MODEL), y_s.reshape(dec_batch, dec_seq, D_MODEL),
            new_k_p, new_v_p, new_u_p, new_k_s, new_v_s, new_u_s)
```

```python
import functools

import jax
import jax.numpy as jnp
from jax import lax
from jax.experimental import pallas as pl
from jax.experimental.pallas import tpu as pltpu

F32 = jnp.float32
BF16 = jnp.bfloat16

D_MODEL = 2048
N_META = 16
N_HEADS = 32
N_KV_HEADS = 8
HEAD_DIM = 64
GROUP = N_HEADS // N_KV_HEADS
WINDOW = 128
ATTN_WIDTH = N_HEADS * HEAD_DIM
KV_WIDTH = N_KV_HEADS * HEAD_DIM
POOL_WIDTH = D_MODEL // 2
POOL_WINDOWS = (2, 4, 8, 16)
POOL_GROUP_DIM = POOL_WIDTH // len(POOL_WINDOWS)
POOL_BUF = max(POOL_WINDOWS) - 1
N_EXPERTS = 32
TOP_K = 4
D_FF = D_MODEL
SWIGLU_LIMIT = 7.0
SWIGLU_ALPHA = 1.702
EPS = 1e-5
IN_COLS = ATTN_WIDTH + 2 * KV_WIDTH + POOL_WIDTH + 2 * D_MODEL

LANES = 128
SUBLANES = 8
VMEM_LIMIT_BYTES = 58 * 1024 * 1024

ATTN_TILE = WINDOW
PROJ_TM = 512
PROJ_TN = 2048
ROW_TM = 256
SAMPLE_SEQS = 4
MOE_ROWS = 128
MOE_SUB = 2 * MOE_ROWS
MOE_MAXM = 12 * MOE_ROWS
MOE_FA = 512
MOE_FB = 512
MOE_GATHER = 2 * MOE_ROWS
MOE_KSLABS = 4
MOE_NFA = D_FF // MOE_FA
MOE_NFB = D_MODEL // MOE_FB
STAGE_PITCH = 3 * SUBLANES
NEG_BIG = -1e30

_SLOPES = tuple(float(2.0 ** (-8.0 * (i + 1) / N_HEADS)) for i in range(N_HEADS))


def _cparams(sem, vmem=VMEM_LIMIT_BYTES):
    return pltpu.CompilerParams(dimension_semantics=sem, vmem_limit_bytes=vmem)


def _rmsnorm(x, g):
    ms = jnp.mean(x * x, axis=-1, keepdims=True)
    return x * lax.rsqrt(ms + EPS) * g


def _prenorm_kernel(xp_ref, xt_ref, g_ref, h_ref, *, n_main):
    x = jnp.where(pl.program_id(0) < n_main, xp_ref[...], xt_ref[...])
    h_ref[...] = _rmsnorm(x, g_ref[...]).astype(h_ref.dtype)


def _prenorm(x_main, x_tail, g_mix):
    n_main = x_main.shape[0] // PROJ_TM
    n_rows = x_main.shape[0] + x_tail.shape[0]
    return pl.pallas_call(
        functools.partial(_prenorm_kernel, n_main=n_main),
        grid=(n_main + 1,),
        in_specs=[
            pl.BlockSpec((PROJ_TM, D_MODEL), lambda m: (jnp.minimum(m, n_main - 1), 0)),
            pl.BlockSpec((PROJ_TM, D_MODEL), lambda m: (0, 0)),
            pl.BlockSpec((1, D_MODEL), lambda m: (0, 0)),
        ],
        out_specs=pl.BlockSpec((PROJ_TM, D_MODEL), lambda m: (m, 0)),
        out_shape=jax.ShapeDtypeStruct((n_rows, D_MODEL), BF16),
        compiler_params=_cparams(("arbitrary",)),
        name="prenorm",
    )(x_main, x_tail, g_mix.reshape(1, D_MODEL))


def _inproj_kernel(h_ref, w_ref, o_ref, wbf_ref):
    @pl.when(pl.program_id(1) == 0)
    def _():
        wbf_ref[...] = w_ref[...].astype(BF16)

    o_ref[...] = jnp.dot(h_ref[...], wbf_ref[...], preferred_element_type=F32)


def _inproj(h, w_in):
    n_rows = h.shape[0]
    grid = (IN_COLS // PROJ_TN, n_rows // PROJ_TM)
    return pl.pallas_call(
        _inproj_kernel,
        grid=grid,
        in_specs=[
            pl.BlockSpec((PROJ_TM, D_MODEL), lambda n, m: (m, 0)),
            pl.BlockSpec((D_MODEL, PROJ_TN), lambda n, m: (0, n)),
        ],
        out_specs=pl.BlockSpec((PROJ_TM, PROJ_TN), lambda n, m: (m, n)),
        out_shape=jax.ShapeDtypeStruct((n_rows, IN_COLS), F32),
        scratch_shapes=[pltpu.VMEM((D_MODEL, PROJ_TN), BF16)],
        compiler_params=_cparams(("arbitrary", "arbitrary")),
        name="inproj",
    )(h, w_in)


def _attn_bias(n_q, n_keys, first_key):
    r = jnp.arange(n_q, dtype=jnp.int32)[:, None]
    c = jnp.arange(n_keys, dtype=jnp.int32)[None, :]
    dist = r + WINDOW - c
    valid = (dist >= 0) & (dist <= WINDOW) & (c >= first_key)
    slopes = jnp.asarray(_SLOPES, F32).reshape(N_KV_HEADS, GROUP, 1, 1)
    bias = jnp.where(valid[None, None], -slopes * dist.astype(F32)[None, None], NEG_BIG)
    return bias.reshape(N_KV_HEADS, GROUP * n_q, n_keys)


def _group_sinks(sink_ref, kvh, n_q):
    g_row = lax.broadcasted_iota(jnp.int32, (GROUP * n_q, 1), 0) // n_q
    sink = jnp.zeros((GROUP * n_q, 1), F32)
    for g in range(GROUP):
        sink = jnp.where(g_row == g, sink_ref[kvh * GROUP + g], sink)
    return sink


def _group_attention(q, k, v, bias, sink_ref, kvh, n_q):
    kh = k[:, kvh * HEAD_DIM:(kvh + 1) * HEAD_DIM]
    vh = v[:, kvh * HEAD_DIM:(kvh + 1) * HEAD_DIM]
    heads = [kvh * GROUP + g for g in range(GROUP)]
    qg = jnp.concatenate([q[:, hd * HEAD_DIM:(hd + 1) * HEAD_DIM] for hd in heads], axis=0)
    s = lax.dot_general(qg, kh, (((1,), (1,)), ((), ())), preferred_element_type=F32) + bias
    sink = _group_sinks(sink_ref, kvh, n_q)
    m = jnp.maximum(jnp.max(s, axis=-1, keepdims=True), sink)
    p = jnp.exp(s - m)
    den = jnp.sum(p, axis=-1, keepdims=True) + jnp.exp(sink - m)
    o = jnp.dot(p.astype(BF16), vh, preferred_element_type=F32)
    return o / den


def _pool_features(ext, n_halo):
    outs = []
    for g, w in enumerate(POOL_WINDOWS):
        e = ext[:, g * POOL_GROUP_DIM:(g + 1) * POOL_GROUP_DIM]
        s = e
        shift = 1
        while shift < w:
            s = s + pltpu.roll(s, shift, axis=0)
            shift *= 2
        outs.append(s[n_halo:] * (1.0 / w) - e[n_halo:])
    return jnp.concatenate(outs, axis=1)


def _attn_prompt_kernel(sink_ref, bias_ref, q_ref, ko_ref, kp_ref, vo_ref, vp_ref, uo_ref, up_ref,
                        o_ref, z_ref, s_scr, p_scr, *, n_prompt_tiles):
    t = ATTN_TILE
    step = pl.program_id(0)

    @pl.when(step < n_prompt_tiles)
    def _():
        q = (q_ref[...] * (HEAD_DIM ** -0.5)).astype(BF16)
        k = jnp.concatenate([kp_ref[...], ko_ref[...]], axis=0).astype(BF16)
        v = jnp.concatenate([vp_ref[...], vo_ref[...]], axis=0).astype(BF16)
        for kvh in range(N_KV_HEADS):
            kh = k[:, kvh * HEAD_DIM:(kvh + 1) * HEAD_DIM]
            qg = jnp.concatenate([q[:, (kvh * GROUP + g) * HEAD_DIM:(kvh * GROUP + g + 1) * HEAD_DIM]
                                  for g in range(GROUP)], axis=0)
            s_scr[kvh] = lax.dot_general(qg, kh, (((1,), (1,)), ((), ())),
                                         preferred_element_type=F32) + bias_ref[0, kvh]
        for kvh in range(N_KV_HEADS):
            s = s_scr[kvh]
            sink = _group_sinks(sink_ref, kvh, t)
            m = jnp.maximum(jnp.max(s, axis=-1, keepdims=True), sink)
            p = jnp.exp(s - m)
            den = jnp.sum(p, axis=-1, keepdims=True) + jnp.exp(sink - m)
            p_scr[kvh] = (p / den).astype(BF16)
        for kvh in range(N_KV_HEADS):
            vh = v[:, kvh * HEAD_DIM:(kvh + 1) * HEAD_DIM]
            o = jnp.dot(p_scr[kvh], vh, preferred_element_type=F32)
            for g in range(GROUP):
                hd = kvh * GROUP + g
                o_ref[:, hd * HEAD_DIM:(hd + 1) * HEAD_DIM] = o[g * t:(g + 1) * t].astype(o_ref.dtype)

        ext = jnp.concatenate([up_ref[...], uo_ref[...]], axis=0)
        z_ref[...] = _pool_features(ext, N_META).astype(z_ref.dtype)

    @pl.when(step >= n_prompt_tiles)
    def _():
        o_ref[...] = jnp.zeros_like(o_ref)
        z_ref[...] = jnp.zeros_like(z_ref)


def _attn_prompt(p_all, sinks, batch, seq, n_out_rows):
    t = ATTN_TILE
    tiles = seq // t
    n_prompt_tiles = batch * tiles
    assert n_out_rows % t == 0
    meta_blk = p_all.shape[0] // t - 1
    q_w, kv_w, u_w = ATTN_WIDTH, KV_WIDTH, POOL_WIDTH
    kcol, vcol, ucol = ATTN_WIDTH // kv_w, ATTN_WIDTH // kv_w + 1, (ATTN_WIDTH + 2 * kv_w) // u_w
    halo_per_tile = t // N_META
    bias = jnp.stack([_attn_bias(t, 2 * t, t - N_META), _attn_bias(t, 2 * t, 0)])

    def prev(s):
        return jnp.where(s % tiles > 0, s - 1, meta_blk)

    return pl.pallas_call(
        functools.partial(_attn_prompt_kernel, n_prompt_tiles=n_prompt_tiles),
        grid=(n_out_rows // t,),
        in_specs=[
            pl.BlockSpec(memory_space=pltpu.SMEM),
            pl.BlockSpec((1, N_KV_HEADS, GROUP * t, 2 * t), lambda s: (jnp.minimum(s % tiles, 1), 0, 0, 0)),
            pl.BlockSpec((t, q_w), lambda s: (s, 0)),
            pl.BlockSpec((t, kv_w), lambda s: (s, kcol)),
            pl.BlockSpec((t, kv_w), lambda s: (prev(s), kcol)),
            pl.BlockSpec((t, kv_w), lambda s: (s, vcol)),
            pl.BlockSpec((t, kv_w), lambda s: (prev(s), vcol)),
            pl.BlockSpec((t, u_w), lambda s: (s, ucol)),
            pl.BlockSpec((N_META, u_w), lambda s: (prev(s) * halo_per_tile + halo_per_tile - 1, ucol)),
        ],
        out_specs=[
            pl.BlockSpec((t, q_w), lambda s: (s, 0)),
            pl.BlockSpec((t, u_w), lambda s: (s, 0)),
        ],
        out_shape=[
            jax.ShapeDtypeStruct((n_out_rows, q_w), BF16),
            jax.ShapeDtypeStruct((n_out_rows, u_w), BF16),
        ],
        scratch_shapes=[
            pltpu.VMEM((N_KV_HEADS, GROUP * t, 2 * t), F32),
            pltpu.VMEM((N_KV_HEADS, GROUP * t, 2 * t), BF16),
        ],
        compiler_params=_cparams(("arbitrary",)),
        name="attn_prompt",
    )(sinks, bias, p_all, p_all, p_all, p_all, p_all, p_all, p_all)


def _attn_sample_kernel(sink_ref, bias_ref, q_ref, kn_ref, vn_ref, un_ref, ck_ref, cv_ref, sp_ref,
                        o_prev_ref, z_prev_ref, o_ref, z_ref):
    del o_prev_ref, z_prev_ref
    n_seqs = ck_ref.shape[0]
    n_new = q_ref.shape[0] // n_seqs
    q_all = (q_ref[...] * (HEAD_DIM ** -0.5)).astype(BF16)
    for si in range(n_seqs):
        rows = slice(si * n_new, (si + 1) * n_new)
        q = q_all[rows]
        k = jnp.concatenate([ck_ref[si], kn_ref[rows, :]], axis=0).astype(BF16)
        v = jnp.concatenate([cv_ref[si], vn_ref[rows, :]], axis=0).astype(BF16)
        for kvh in range(N_KV_HEADS):
            o = _group_attention(q, k, v, bias_ref[kvh], sink_ref, kvh, n_new)
            for g in range(GROUP):
                hd = kvh * GROUP + g
                o_ref[rows, hd * HEAD_DIM:(hd + 1) * HEAD_DIM] = (
                    o[g * n_new:(g + 1) * n_new].astype(o_ref.dtype))
        ext = jnp.concatenate([sp_ref[si], un_ref[rows, :]], axis=0)
        z_ref[rows, :] = _pool_features(ext, sp_ref.shape[1]).astype(z_ref.dtype)


def _attn_sample(p_all, sinks, cache_k, cache_v, state_pad, o_attn, z, row0, dec_batch, dec_seq):
    q_w, kv_w, u_w = ATTN_WIDTH, KV_WIDTH, POOL_WIDTH
    kcol, vcol, ucol = ATTN_WIDTH // kv_w, ATTN_WIDTH // kv_w + 1, (ATTN_WIDTH + 2 * kv_w) // u_w
    ns = SAMPLE_SEQS
    rows = ns * dec_seq
    blk0 = row0 // rows
    n_halo = state_pad.shape[1]
    n_keys = WINDOW + dec_seq
    bias = _attn_bias(dec_seq, n_keys, 0)
    return pl.pallas_call(
        _attn_sample_kernel,
        grid=(dec_batch // ns,),
        in_specs=[
            pl.BlockSpec(memory_space=pltpu.SMEM),
            pl.BlockSpec((N_KV_HEADS, GROUP * dec_seq, n_keys), lambda b: (0, 0, 0)),
            pl.BlockSpec((rows, q_w), lambda b: (blk0 + b, 0)),
            pl.BlockSpec((rows, kv_w), lambda b: (blk0 + b, kcol)),
            pl.BlockSpec((rows, kv_w), lambda b: (blk0 + b, vcol)),
            pl.BlockSpec((rows, u_w), lambda b: (blk0 + b, ucol)),
            pl.BlockSpec((ns, WINDOW, kv_w), lambda b: (b, 0, 0)),
            pl.BlockSpec((ns, WINDOW, kv_w), lambda b: (b, 0, 0)),
            pl.BlockSpec((ns, n_halo, u_w), lambda b: (b, 0, 0)),
            pl.BlockSpec(memory_space=pl.ANY),
            pl.BlockSpec(memory_space=pl.ANY),
        ],
        out_specs=[
            pl.BlockSpec((rows, q_w), lambda b: (blk0 + b, 0)),
            pl.BlockSpec((rows, u_w), lambda b: (blk0 + b, 0)),
        ],
        out_shape=[
            jax.ShapeDtypeStruct(o_attn.shape, o_attn.dtype),
            jax.ShapeDtypeStruct(z.shape, z.dtype),
        ],
        input_output_aliases={9: 0, 10: 1},
        compiler_params=_cparams(("arbitrary",)),
        name="attn_sample",
    )(sinks, bias, p_all, p_all, p_all, p_all, cache_k, cache_v, state_pad, o_attn, z)


def _finish_kernel(xp_ref, xs_ref, oa_ref, z_ref, ag_ref, pg_ref, wpm_ref, ps_ref, wba_ref, wbp_ref,
                   wo_ref, gf_ref, wr_ref, br_ref,
                   x1_ref, h2_ref, idx_ref, gate_ref, rank_ref, cnt_ref, carry_ref,
                   *, n_prompt_tiles):
    tm = xp_ref.shape[0]
    i = pl.program_id(0)

    @pl.when(i == 0)
    def _():
        carry_ref[...] = jnp.zeros_like(carry_ref)

    x_in = jnp.where(i < n_prompt_tiles, xp_ref[...], xs_ref[...])

    z = z_ref[...]
    zp = jnp.concatenate(
        [jnp.dot(z[:, g * POOL_GROUP_DIM:(g + 1) * POOL_GROUP_DIM], wpm_ref[g],
                 preferred_element_type=F32) for g in range(len(POOL_WINDOWS))], axis=1)
    zp = (zp * ps_ref[...]).astype(BF16)
    ya = jnp.dot(oa_ref[...], wba_ref[...], preferred_element_type=F32)
    yp = jnp.dot(zp, wbp_ref[...], preferred_element_type=F32)
    merged = jax.nn.sigmoid(ag_ref[...]) * ya + jax.nn.sigmoid(pg_ref[...]) * yp
    x1 = x_in + jnp.dot(merged.astype(BF16), wo_ref[...], preferred_element_type=F32)
    x1_ref[...] = x1
    h2 = _rmsnorm(x1, gf_ref[...])
    n_lc = D_MODEL // LANES
    for c in range(n_lc):
        h2_ref[pl.ds(c, tm, stride=n_lc), :] = h2[:, c * LANES:(c + 1) * LANES]

    h_hi = h2.astype(BF16)
    h_lo = (h2 - h_hi.astype(F32)).astype(BF16)
    t = jnp.dot(h_hi, wr_ref[...], preferred_element_type=F32)
    logits = (t[:, :N_EXPERTS] + t[:, N_EXPERTS:]
              + jnp.dot(h_lo, wr_ref[:, :N_EXPERTS], preferred_element_type=F32) + br_ref[...])

    col = lax.broadcasted_iota(jnp.int32, logits.shape, 1).astype(F32)
    vals, idxs = [], []
    for _ in range(TOP_K):
        m = jnp.max(logits, axis=-1, keepdims=True)
        idx = jnp.min(jnp.where(logits == m, col, float(N_EXPERTS)), axis=-1, keepdims=True)
        vals.append(m)
        idxs.append(idx)
        logits = jnp.where(col == idx, -jnp.inf, logits)
    exps = [jnp.exp(v - vals[0]) for v in vals]
    den = exps[0] + exps[1] + exps[2] + exps[3]

    member = jnp.zeros(logits.shape, F32)
    for k in range(TOP_K):
        member = member + (col == idxs[k]).astype(F32)
    ri = lax.broadcasted_iota(jnp.int32, (tm, tm), 0)
    ci = lax.broadcasted_iota(jnp.int32, (tm, tm), 1)
    earlier = (ri > ci).astype(BF16)
    before = jnp.dot(earlier, member.astype(BF16), preferred_element_type=F32) + carry_ref[...]
    for k in range(TOP_K):
        idx_ref[:, k:k + 1] = idxs[k].astype(jnp.int32)
        gate_ref[:, k:k + 1] = exps[k] / den
        rank_ref[:, k:k + 1] = jnp.sum(jnp.where(col == idxs[k], before, 0.0), axis=-1,
                                       keepdims=True).astype(jnp.int32)
    carry_ref[...] += jnp.sum(member, axis=0, keepdims=True)
    cnt_ref[...] = carry_ref[...]


def _finish(x_prompt, x_sample, o_attn, z, p_all, wpm, pool_scale, wba, wbp, wo, g_ffn, w_router2,
            b_router):
    n_tok = o_attn.shape[0]
    tm = ROW_TM
    n_prompt_tiles = x_prompt.shape[0] // tm
    n_sample_tiles = x_sample.shape[0] // tm
    acol, pcol = (IN_COLS - 2 * D_MODEL) // D_MODEL, (IN_COLS - D_MODEL) // D_MODEL
    const = pl.Buffered(1)

    def whole(shape):
        nd = len(shape)
        return pl.BlockSpec(shape, lambda i: (0,) * nd, pipeline_mode=const)

    return pl.pallas_call(
        functools.partial(_finish_kernel, n_prompt_tiles=n_prompt_tiles),
        grid=(n_tok // tm,),
        in_specs=[
            pl.BlockSpec((tm, D_MODEL), lambda i: (jnp.minimum(i, n_prompt_tiles - 1), 0)),
            pl.BlockSpec((tm, D_MODEL),
                         lambda i: (jnp.clip(i - n_prompt_tiles, 0, n_sample_tiles - 1), 0)),
            pl.BlockSpec((tm, ATTN_WIDTH), lambda i: (i, 0)),
            pl.BlockSpec((tm, POOL_WIDTH), lambda i: (i, 0)),
            pl.BlockSpec((tm, D_MODEL), lambda i: (i, acol)),
            pl.BlockSpec((tm, D_MODEL), lambda i: (i, pcol)),
            whole(wpm.shape), whole((1, POOL_WIDTH)), whole(wba.shape), whole(wbp.shape),
            whole(wo.shape), whole((1, D_MODEL)), whole(w_router2.shape), whole((1, N_EXPERTS)),
        ],
        out_specs=[
            pl.BlockSpec((tm, D_MODEL), lambda i: (i, 0)),
            pl.BlockSpec((tm * (D_MODEL // LANES), LANES), lambda i: (i, 0)),
            pl.BlockSpec((tm, TOP_K), lambda i: (i, 0)),
            pl.BlockSpec((tm, TOP_K), lambda i: (i, 0)),
            pl.BlockSpec((tm, TOP_K), lambda i: (i, 0)),
            pl.BlockSpec((1, N_EXPERTS), lambda i: (0, 0)),
        ],
        out_shape=[
            jax.ShapeDtypeStruct((n_tok, D_MODEL), F32),
            jax.ShapeDtypeStruct((n_tok * (D_MODEL // LANES), LANES), F32),
            jax.ShapeDtypeStruct((n_tok, TOP_K), jnp.int32),
            jax.ShapeDtypeStruct((n_tok, TOP_K), F32),
            jax.ShapeDtypeStruct((n_tok, TOP_K), jnp.int32),
            jax.ShapeDtypeStruct((1, N_EXPERTS), F32),
        ],
        scratch_shapes=[pltpu.VMEM((1, N_EXPERTS), F32)],
        compiler_params=_cparams(("arbitrary",)),
        name="finish",
    )(x_prompt, x_sample, o_attn, z, p_all, p_all, wpm, pool_scale.reshape(1, POOL_WIDTH), wba, wbp,
      wo, g_ffn.reshape(1, D_MODEL), w_router2, b_router.reshape(1, N_EXPERTS))


def _route(top_idx, rank, counts_f, n_seg):
    counts = counts_f.reshape(N_EXPERTS).astype(jnp.int32)
    padded = (counts + MOE_ROWS - 1) // MOE_ROWS * MOE_ROWS
    pend = jnp.cumsum(padded)
    pstart = pend - padded
    experts = jnp.arange(N_EXPERTS, dtype=jnp.int32)
    start_of = jnp.sum(jnp.where(top_idx[..., None] == experts, pstart, 0), axis=-1)
    dest = (start_of + rank).reshape(-1)
    _, tok_order = lax.sort_key_val(dest, jnp.arange(dest.shape[0], dtype=jnp.int32) // TOP_K)
    tok_sorted = jnp.concatenate([tok_order, jnp.zeros((MOE_GATHER,), jnp.int32)])
    cstart = jnp.cumsum(counts) - counts

    nseg_e = (counts + MOE_MAXM - 1) // MOE_MAXM
    seg_end = jnp.cumsum(nseg_e)
    seg_base = seg_end - nseg_e
    s_ids = jnp.arange(n_seg, dtype=jnp.int32)
    last = jnp.maximum(seg_end[-1] - 1, 0)
    s_eff = jnp.minimum(s_ids, last)
    e_of_s = jnp.minimum(jnp.sum(s_eff[:, None] >= seg_end[None, :], axis=1), N_EXPERTS - 1)
    e_prev = jnp.concatenate([e_of_s[:1], e_of_s[:-1]])
    is_last = s_ids >= last
    e_next = jnp.where(is_last, e_of_s, jnp.concatenate([e_of_s[1:], e_of_s[-1:]]))
    c_next = jnp.where(is_last, MOE_NFA - 1, 0)
    k_in = s_eff - seg_base[e_of_s]
    nrows = jnp.clip(counts[e_of_s] - k_in * MOE_MAXM, 0, MOE_MAXM)
    n_gran = (nrows + MOE_ROWS - 1) // MOE_ROWS
    row0 = pstart[e_of_s] + k_in * MOE_MAXM
    tok0 = cstart[e_of_s] + k_in * MOE_MAXM
    total_gran = (pend[-1] // MOE_ROWS).reshape(1)
    i32 = jnp.int32
    n_live = (last + 1).astype(i32)
    return (n_live, e_of_s.astype(i32), e_prev.astype(i32), e_next.astype(i32), c_next.astype(i32),
            row0.astype(i32), tok0.astype(i32), n_gran.astype(i32), total_gran.astype(i32), tok_sorted,
            dest.astype(i32))


def _for_blocks(base, n_gran, fn):
    n_full = n_gran // 2

    def pair(p, carry):
        r = base + p * (2 * MOE_SUB)
        fn(r, MOE_SUB)
        fn(r + MOE_SUB, MOE_SUB)
        return carry
    lax.fori_loop(0, n_full // 2, pair, 0)

    @pl.when(n_full % 2 == 1)
    def _():
        fn(base + (n_full - 1) * MOE_SUB, MOE_SUB)

    @pl.when(n_gran % 2 == 1)
    def _():
        fn(base + n_full * MOE_SUB, MOE_ROWS)


def _moe_kernel(seg_e, seg_ep, seg_en, seg_cn, seg_row0, seg_tok0, seg_ngran, total_gran, tok_smem,
                h2_hbm, wg_ref, wl_ref, wd_ref, bg_ref, bl_ref, bd_ref,
                y_hbm,
                xbuf, actbuf, stage, wab, wdb, ostage, pend, gsem, osem):
    del seg_e, seg_ep, seg_en, seg_cn
    s = pl.program_id(0)
    j = pl.program_id(1)
    n_gran = seg_ngran[s]
    row0 = seg_row0[s]
    live = n_gran > 0
    gran = MOE_ROWS
    fa, fb = MOE_FA, MOE_FB
    n_lc = D_MODEL // LANES
    kslab = D_MODEL // MOE_KSLABS

    def rows_at(first, n_rows):
        return pl.ds(pl.multiple_of(first, MOE_ROWS), n_rows)

    gch = MOE_GATHER

    def issue(tok0, chunk, slot):
        base = tok0 + chunk * gch

        def body(r, carry):
            tok = tok_smem[base + r]
            pltpu.make_async_copy(
                h2_hbm.at[pl.ds(pl.multiple_of(tok * n_lc, n_lc), n_lc)],
                stage.at[slot, pl.ds(pl.multiple_of(r * STAGE_PITCH, SUBLANES), n_lc)],
                gsem.at[slot]).start()
            return carry
        lax.fori_loop(0, gch, body, 0, unroll=8)

    def land(chunk, slot):
        pltpu.make_async_copy(h2_hbm.at[pl.ds(0, gch * n_lc)], stage.at[slot, pl.ds(0, gch * n_lc)],
                              gsem.at[slot]).wait()
        rows = rows_at(chunk * gch, gch)
        for lc in range(n_lc):
            xbuf[rows, lc * LANES:(lc + 1) * LANES] = (
                stage[slot, pl.ds(lc, gch, stride=STAGE_PITCH), :].astype(BF16))

    def n_chunks(n_gran_):
        return (n_gran_ * gran + gch - 1) // gch

    @pl.when(jnp.logical_and(s == 0, j == 0))
    def _first():
        pend[0] = 0
        pend[1] = 0

        nc = n_chunks(n_gran)
        issue(seg_tok0[s], 0, 0)

        def chunk_body(c, carry):
            slot = c % 2

            @pl.when(c + 1 < nc)
            def _():
                issue(seg_tok0[s], c + 1, 1 - slot)
            land(c, slot)
            return carry
        lax.fori_loop(0, nc, chunk_body, 0)

    def prefetch_next_rows(jb):
        has_next = s + 1 < pl.num_programs(0)
        s_next = jnp.minimum(s + 1, pl.num_programs(0) - 1)
        nc = jnp.where(has_next, n_chunks(seg_ngran[s_next]), 0)
        next_tok0 = seg_tok0[s_next]
        for q in range(2):
            @pl.when(jnp.logical_and(jb >= 1, 2 * (jb - 1) + q < nc))
            def _():
                land(2 * (jb - 1) + q, q)
        for q in range(2):
            @pl.when(jnp.logical_and(jb + 1 < MOE_NFB, 2 * jb + q < nc))
            def _():
                issue(next_tok0, 2 * jb + q, q)

    @pl.when(jnp.logical_and(j < MOE_NFA, live))
    def _up():
        bias = jnp.concatenate([bg_ref[0], bl_ref[0]], axis=1)

        def activate(rows, gu):
            glu = jnp.minimum(gu[:, :fa], SWIGLU_LIMIT)
            lin = jnp.clip(gu[:, fa:], -SWIGLU_LIMIT, SWIGLU_LIMIT)
            actbuf[j, rows, :] = (glu * jax.nn.sigmoid(SWIGLU_ALPHA * glu) * (lin + 1.0)).astype(BF16)

        rows0 = pl.ds(0, gran)
        gu = bias
        for c in range(MOE_KSLABS):
            ks = slice(c * kslab, (c + 1) * kslab)
            wab[ks, :fa] = wg_ref[0, ks, :].astype(BF16)
            wab[ks, fa:] = wl_ref[0, ks, :].astype(BF16)
            gu = gu + jnp.dot(xbuf[rows0, ks], wab[ks, :], preferred_element_type=F32)
        activate(rows0, gu)

        def one(first, n_rows):
            rows = rows_at(first, n_rows)
            activate(rows, jnp.dot(xbuf[rows, :], wab[...], preferred_element_type=F32) + bias)
        _for_blocks(gran, n_gran - 1, one)

    def out_copy(slot, src_row, dst_row, col0):
        return pltpu.make_async_copy(ostage.at[slot, rows_at(src_row, gran)],
                                     y_hbm.at[rows_at(dst_row, gran), pl.ds(col0, fb)], osem.at[slot])

    def out_wait(slot):
        def body(i, carry):
            out_copy(slot, 0, 0, 0).wait()
            return carry
        lax.fori_loop(0, pend[slot], body, 0)
        pend[slot] = 0

    @pl.when(jnp.logical_and(j >= MOE_NFA, live))
    def _down():
        jb = j - MOE_NFA
        slot = jb % 2
        col0 = pl.multiple_of(jb * fb, fb)
        prefetch_next_rows(jb)
        out_wait(slot)

        rows0 = pl.ds(0, gran)
        yd = bd_ref[0]
        for c in range(MOE_NFA):
            ks = slice(c * fa, (c + 1) * fa)
            wdb[ks, :] = wd_ref[0, ks, :].astype(BF16)
            yd = yd + jnp.dot(actbuf[c, rows0, :], wdb[ks, :], preferred_element_type=F32)
        ostage[slot, rows0, :] = yd
        out_copy(slot, 0, row0, col0).start()

        def one(first, n_rows):
            rows = rows_at(first, n_rows)
            act = jnp.concatenate([actbuf[c, rows, :] for c in range(MOE_NFA)], axis=1)
            ostage[slot, rows, :] = jnp.dot(act, wdb[...], preferred_element_type=F32) + bd_ref[0]
            for h in range(n_rows // gran):
                out_copy(slot, first + h * gran, row0 + first + h * gran, col0).start()
        _for_blocks(gran, n_gran - 1, one)
        pend[slot] = n_gran

    @pl.when(jnp.logical_and(s == pl.num_programs(0) - 1, j == pl.num_programs(1) - 1))
    def _final():
        out_wait(0)
        out_wait(1)
        n_tail = y_hbm.shape[0] // gran - total_gran[0]

        @pl.when(n_tail > 0)
        def _():
            ostage[0, pl.ds(0, gran), :] = jnp.zeros((gran, fb), F32)

            def fill(t, carry):
                for cb in range(MOE_NFB):
                    out_copy(0, 0, (total_gran[0] + t) * gran, cb * fb).start()
                return carry
            lax.fori_loop(0, n_tail, fill, 0)
            pend[0] = n_tail * MOE_NFB
            out_wait(0)


def _moe(h2, n_live, seg_e, seg_ep, seg_en, seg_cn, seg_row0, seg_tok0, seg_ngran, total_gran, tok_sorted,
         w_gate_up, b_gate_up, w_down, b_down, cap):
    nfa, nfb = MOE_NFA, MOE_NFB
    assert MOE_MAXM <= 2 * (nfb - 1) * MOE_GATHER

    def up_block(s, j, e, en, cn):
        ahead = j >= nfa + nfb // 2
        eb = jnp.where(ahead, en[s], e[s])
        cb = jnp.where(j < nfa, j, jnp.where(ahead, cn[s], nfa - 1))
        return eb, cb

    def wg_map(s, j, e, ep, en, cn, r0, t0, ng, tg, tok):
        eb, cb = up_block(s, j, e, en, cn)
        return (eb, 0, cb)

    def wl_map(s, j, e, ep, en, cn, r0, t0, ng, tg, tok):
        eb, cb = up_block(s, j, e, en, cn)
        return (eb, 0, nfa + cb)

    def wd_map(s, j, e, ep, en, cn, r0, t0, ng, tg, tok):
        in_up = j < nfa
        eb = jnp.where(in_up, ep[s], e[s])
        jb = jnp.where(in_up, jnp.where(s > 0, nfb - 1, 0), j - nfa)
        return (eb, 0, jb)

    grid_spec = pltpu.PrefetchScalarGridSpec(
        num_scalar_prefetch=9,
        grid=(n_live, nfa + nfb),
        in_specs=[
            pl.BlockSpec(memory_space=pl.ANY),
            pl.BlockSpec((1, D_MODEL, MOE_FA), wg_map),
            pl.BlockSpec((1, D_MODEL, MOE_FA), wl_map),
            pl.BlockSpec((1, D_FF, MOE_FB), wd_map),
            pl.BlockSpec((1, 1, MOE_FA), wg_map),
            pl.BlockSpec((1, 1, MOE_FA), wl_map),
            pl.BlockSpec((1, 1, MOE_FB), wd_map),
        ],
        out_specs=pl.BlockSpec(memory_space=pl.ANY),
        scratch_shapes=[
            pltpu.VMEM((MOE_MAXM, D_MODEL), BF16),
            pltpu.VMEM((nfa, MOE_MAXM, MOE_FA), BF16),
            pltpu.VMEM((2, MOE_GATHER * STAGE_PITCH, LANES), F32),
            pltpu.VMEM((D_MODEL, 2 * MOE_FA), BF16),
            pltpu.VMEM((D_FF, MOE_FB), BF16),
            pltpu.VMEM((2, MOE_MAXM, MOE_FB), F32),
            pltpu.SMEM((2,), jnp.int32),
            pltpu.SemaphoreType.DMA((2,)),
            pltpu.SemaphoreType.DMA((2,)),
        ],
    )
    return pl.pallas_call(
        _moe_kernel,
        grid_spec=grid_spec,
        out_shape=jax.ShapeDtypeStruct((cap, D_MODEL), F32),
        compiler_params=_cparams(("arbitrary", "arbitrary")),
        name="moe",
    )(seg_e, seg_ep, seg_en, seg_cn, seg_row0, seg_tok0, seg_ngran, total_gran, tok_sorted,
      h2, w_gate_up, w_gate_up, w_down,
      b_gate_up.reshape(N_EXPERTS, 1, 2 * D_FF), b_gate_up.reshape(N_EXPERTS, 1, 2 * D_FF),
      b_down.reshape(N_EXPERTS, 1, D_MODEL))


def _combine_kernel(pos_ref, y_hbm, x1_ref, gate_ref, gfin_ref, op_ref, os_ref, buf, sem,
                    *, n_prompt_tiles):
    i = pl.program_id(0)
    n = pl.num_programs(0)
    tm = ROW_TM

    def row_copy(tile, slot, r, k):
        p = pos_ref[(tile * tm + r) * TOP_K + k]
        return pltpu.make_async_copy(y_hbm.at[pl.ds(p, 1)], buf.at[slot, k, pl.ds(r, 1)],
                                     sem.at[slot])

    def issue(tile, slot):
        def body(r, carry):
            for k in range(TOP_K):
                row_copy(tile, slot, r, k).start()
            return carry
        lax.fori_loop(0, tm, body, 0, unroll=4)

    def wait(slot):
        for k in range(TOP_K):
            pltpu.make_async_copy(y_hbm.at[pl.ds(0, tm)], buf.at[slot, k], sem.at[slot]).wait()

    slot = i % 2

    @pl.when(i == 0)
    def _():
        issue(0, 0)

    for parity in range(2):
        @pl.when(jnp.logical_and(i + 1 < n, slot == parity))
        def _():
            issue(i + 1, 1 - parity)

    wait(slot)
    gate = gate_ref[...]
    x2 = x1_ref[...]
    for k in range(TOP_K):
        x2 = x2 + gate[:, k:k + 1] * buf[slot, k]
    ms = jnp.mean(x2 * x2, axis=-1, keepdims=True)
    y = x2 * lax.rsqrt(ms + EPS) * gfin_ref[...]

    @pl.when(i < n_prompt_tiles)
    def _():
        op_ref[...] = y

    @pl.when(i >= n_prompt_tiles)
    def _():
        os_ref[...] = y


def _combine(pos, y_sorted, x1, gate, g_final, n_prompt, n_sample):
    tm = ROW_TM
    n_tok = x1.shape[0]
    n_prompt_tiles = n_prompt // tm
    n_sample_tiles = n_sample // tm

    grid_spec = pltpu.PrefetchScalarGridSpec(
        num_scalar_prefetch=1,
        grid=(n_tok // tm,),
        in_specs=[
            pl.BlockSpec(memory_space=pl.ANY),
            pl.BlockSpec((tm, D_MODEL), lambda i, pos: (i, 0)),
            pl.BlockSpec((tm, TOP_K), lambda i, pos: (i, 0)),
            pl.BlockSpec((1, D_MODEL), lambda i, pos: (0, 0)),
        ],
        out_specs=[
            pl.BlockSpec((tm, D_MODEL), lambda i, pos: (jnp.minimum(i, n_prompt_tiles - 1), 0)),
            pl.BlockSpec((tm, D_MODEL),
                         lambda i, pos: (jnp.clip(i - n_prompt_tiles, 0, n_sample_tiles - 1), 0)),
        ],
        scratch_shapes=[
            pltpu.VMEM((2, TOP_K, tm, D_MODEL), F32),
            pltpu.SemaphoreType.DMA((2,)),
        ],
    )
    return pl.pallas_call(
        functools.partial(_combine_kernel, n_prompt_tiles=n_prompt_tiles),
        grid_spec=grid_spec,
        out_shape=[
            jax.ShapeDtypeStruct((n_prompt, D_MODEL), F32),
            jax.ShapeDtypeStruct((n_sample, D_MODEL), F32),
        ],
        compiler_params=_cparams(("arbitrary",)),
        name="combine",
    )(pos, y_sorted, x1, gate, g_final.reshape(1, D_MODEL))


def kernel(x_prompt, x_sample, cache_k, cache_v, state_pool, meta_tokens, g_mix, w_in, sinks,
           w_pool_mix, pool_scale, w_br_attn, w_br_pool, w_out, g_ffn, w_router, b_router,
           w_gate_up, b_gate_up, w_down, b_down, g_final):
    depth = w_in.shape[0]
    assert depth == 1, "single-layer step only"
    batch, seq, _ = x_prompt.shape
    dec_batch, dec_seq, _ = x_sample.shape
    n_prompt = batch * seq
    n_sample = dec_batch * dec_seq
    n_tok = n_prompt + n_sample
    assert seq % ATTN_TILE == 0 and n_prompt % ROW_TM == 0 and n_sample % ROW_TM == 0
    assert dec_seq == SUBLANES and dec_batch % SAMPLE_SEQS == 0

    assert n_prompt % PROJ_TM == 0 and n_sample + ATTN_TILE <= PROJ_TM
    xp = x_prompt.reshape(n_prompt, D_MODEL)
    xs = x_sample.reshape(n_sample, D_MODEL)
    pad = jnp.zeros((PROJ_TM - n_sample - N_META, D_MODEL), F32)
    x_tail = jnp.concatenate([xs, pad, meta_tokens.astype(F32)], axis=0)

    p_all = _inproj(_prenorm(xp, x_tail, g_mix[0]), w_in[0])

    ck = cache_k[0].reshape(dec_batch, WINDOW, KV_WIDTH)
    cv = cache_v[0].reshape(dec_batch, WINDOW, KV_WIDTH)
    state_pad = jnp.pad(state_pool[0], ((0, 0), (N_META - POOL_BUF, 0), (0, 0)))
    o_attn, z = _attn_prompt(p_all, sinks[0], batch, seq, n_tok)
    o_attn, z = _attn_sample(p_all, sinks[0], ck, cv, state_pad, o_attn, z, n_prompt, dec_batch,
                             dec_seq)

    wr = w_router[0]
    wr_hi = wr.astype(BF16)
    wr_lo = (wr - wr_hi.astype(F32)).astype(BF16)
    x1, h2, top_idx, gate, rank, counts = _finish(
        xp, xs, o_attn, z, p_all, w_pool_mix[0].astype(BF16), pool_scale[0],
        w_br_attn[0].astype(BF16), w_br_pool[0].astype(BF16), w_out[0].astype(BF16),
        g_ffn[0], jnp.concatenate([wr_hi, wr_lo], axis=1), b_router[0])

    n_assign = n_tok * TOP_K
    n_seg = N_EXPERTS + n_assign // MOE_MAXM
    cap = (n_assign // MOE_ROWS + N_EXPERTS) * MOE_ROWS
    (n_live, seg_e, seg_ep, seg_en, seg_cn, seg_row0, seg_tok0, seg_ngran, total_gran, tok_sorted,
     dest) = _route(top_idx, rank, counts, n_seg)
    y_sorted = _moe(h2, n_live, seg_e, seg_ep, seg_en, seg_cn, seg_row0, seg_tok0, seg_ngran, total_gran,
                    tok_sorted, w_gate_up[0], b_gate_up[0], w_down[0], b_down[0], cap)
    y_p, y_s = _combine(dest, y_sorted, x1, gate, g_final, n_prompt, n_sample)

    k0, v0, u0, u1 = ATTN_WIDTH, ATTN_WIDTH + KV_WIDTH, ATTN_WIDTH + 2 * KV_WIDTH, IN_COLS - 2 * D_MODEL

    def tail_rows(n, c0, c1):
        return jnp.stack([p_all[(b + 1) * seq - n:(b + 1) * seq, c0:c1] for b in range(batch)])

    new_k_p = tail_rows(WINDOW, k0, v0).reshape(1, batch, WINDOW, N_KV_HEADS, HEAD_DIM)
    new_v_p = tail_rows(WINDOW, v0, u0).reshape(1, batch, WINDOW, N_KV_HEADS, HEAD_DIM)
    new_u_p = tail_rows(POOL_BUF, u0, u1)[None]
    ps = p_all[n_prompt:n_tok, k0:u1].reshape(dec_batch, dec_seq, u1 - k0)
    new_k_s = jnp.concatenate([ck[:, dec_seq:], ps[:, :, :KV_WIDTH]], axis=1).reshape(
        1, dec_batch, WINDOW, N_KV_HEADS, HEAD_DIM)
    new_v_s = jnp.concatenate([cv[:, dec_seq:], ps[:, :, KV_WIDTH:2 * KV_WIDTH]], axis=1).reshape(
        1, dec_batch, WINDOW, N_KV_HEADS, HEAD_DIM)
    new_u_s = jnp.concatenate([state_pool[0][:, dec_seq:], ps[:, :, 2 * KV_WIDTH:]], axis=1)[None]

    return (y_p.reshape(batch, seq, D_MODEL), y_s.reshape(dec_batch, dec_seq, D_MODEL),
            new_k_p, new_v_p, new_u_p, new_k_s, new_v_s, new_u_s)
```
